```python
import math
import jax, jax.numpy as jnp
from jax import lax
import numpy as np

D_MODEL = 2048
BATCH = 2
SEQ = 4096
DEPTH = 2
DEC_BATCH = 128
DEC_SEQ = 1
PAST_LEN = 8192
PAGE_SIZE = 128

MLA_HEADS = 8
MLA_NOPE = 128
MLA_ROPE = 64
MLA_V = 128
MLA_Q_RANK = 512
MLA_KV_RANK = 512
MLA_SCALE = (MLA_NOPE + MLA_ROPE) ** -0.5
ROPE_THETA = 10000.0
RWKV_HEADS = 16
RWKV_HEAD = 64
RWKV_W = RWKV_HEADS * RWKV_HEAD
RWKV_DECAY_RANK = 64
RWKV_A_RANK = 64
RWKV_GATE_RANK = 160
RWKV_GN_EPS = 64e-5
SB_HEADS = 16
SB_HEAD = 64
SB_W = SB_HEADS * SB_HEAD
SB_SCALE = SB_HEAD ** -0.5
GLA_HEADS = 4
GLA_DK = 128
GLA_DV = 256
GLA_GATE_RANK = 16
GLA_GATE_NORM = 16.0
GLA_CHUNK = 64
D_FF = -(-8 * D_MODEL // (3 * 256)) * 256

Q_BLOCK = 128
NORM_EPS = 1e-6

MLA_SPLITS = (MLA_Q_RANK, MLA_KV_RANK, MLA_ROPE)
MLA_COLS = MLA_Q_RANK + MLA_KV_RANK + MLA_ROPE
RWKV_SPLITS = (RWKV_W, RWKV_W, RWKV_W, RWKV_DECAY_RANK, RWKV_A_RANK, RWKV_GATE_RANK)
RWKV_COLS = 3 * RWKV_W + RWKV_DECAY_RANK + RWKV_A_RANK + RWKV_GATE_RANK
IN0_COLS = MLA_COLS + RWKV_COLS
MIX0_W = MLA_HEADS * MLA_V + RWKV_W
SB_COLS = 3 * SB_W
GLA_SPLITS = (GLA_HEADS * GLA_DK, GLA_HEADS * GLA_DK, GLA_HEADS * GLA_DV, GLA_GATE_RANK, GLA_HEADS * GLA_DV)
GLA_COLS = 2 * GLA_HEADS * GLA_DK + 2 * GLA_HEADS * GLA_DV + GLA_GATE_RANK
IN1_COLS = SB_COLS + GLA_COLS
MIX1_W = SB_W + GLA_HEADS * GLA_DV

kernel_name = 'hybrid_mla_rwkv7_stickbreak_gla_decode_step'


def split_cols(x, sizes):
    out, start = [], 0
    for s in sizes:
        out.append(x[..., start:start + s])
        start += s
    return out


def rmsnorm(x, g):
    xf = x.astype(jnp.float32)
    y = xf * lax.rsqrt(jnp.mean(xf * xf, axis=-1, keepdims=True) + NORM_EPS)
    return (y * g.astype(jnp.float32)).astype(x.dtype)


def rope(x, pos):
    half = x.shape[-1] // 2
    inv = 1.0 / (ROPE_THETA ** (jnp.arange(half, dtype=jnp.float32) / half))
    ang = pos.astype(jnp.float32)[:, None] * inv[None, :]
    shape = (pos.shape[0],) + (1,) * (x.ndim - 3) + (half,)
    cos = jnp.cos(ang).reshape(shape)
    sin = jnp.sin(ang).reshape(shape)
    xf = x.astype(jnp.float32)
    x1, x2 = xf[..., :half], xf[..., half:]
    return jnp.concatenate([x1 * cos - x2 * sin, x2 * cos + x1 * sin], axis=-1).astype(x.dtype)


def sweep_query_blocks(block_fn, *qs):
    B, T = qs[0].shape[:2]
    nb = T // Q_BLOCK
    to_blocks = lambda t: jnp.moveaxis(t.reshape(B, nb, Q_BLOCK, *t.shape[2:]), 1, 0)
    starts = jnp.arange(nb, dtype=jnp.int32) * Q_BLOCK
    out = lax.map(lambda args: block_fn(*args), (starts,) + tuple(to_blocks(q) for q in qs))
    out = jnp.moveaxis(out, 0, 1)
    return out.reshape(B, T, *out.shape[3:])


def gather_pages(pool, page_table):
    g = jnp.take(pool, page_table, axis=0)
    return g.reshape(page_table.shape[0], page_table.shape[1] * pool.shape[1], *pool.shape[2:])


def mla_prompt(q_abs, q_pe, ckv, kpe):
    kpos = jnp.arange(ckv.shape[1])
    def block(start, qa, qp):
        s = (jnp.einsum('bqhr,bkr->bhqk', qa, ckv) + jnp.einsum('bqhp,bkp->bhqk', qp, kpe)).astype(jnp.float32) * MLA_SCALE
        qpos = start + jnp.arange(Q_BLOCK)
        s = jnp.where(kpos[None, :] <= qpos[:, None], s, -jnp.inf)
        prob = jax.nn.softmax(s, axis=-1).astype(ckv.dtype)
        return jnp.einsum('bhqk,bkr->bqhr', prob, ckv)
    return sweep_query_blocks(block, q_abs, q_pe)


def mla_sample(q_abs, q_pe, ckv_new, kpe_new, ckv_past, kpe_past):
    Q = q_abs.shape[1]
    P = ckv_past.shape[1]
    s_past = jnp.einsum('bqhr,bkr->bhqk', q_abs, ckv_past) + jnp.einsum('bqhp,bkp->bhqk', q_pe, kpe_past)
    s_new = jnp.einsum('bqhr,bkr->bhqk', q_abs, ckv_new) + jnp.einsum('bqhp,bkp->bhqk', q_pe, kpe_new)
    causal = jnp.arange(Q)[None, :] <= jnp.arange(Q)[:, None]
    s_new = jnp.where(causal, s_new.astype(jnp.float32) * MLA_SCALE, -jnp.inf)
    s = jnp.concatenate([s_past.astype(jnp.float32) * MLA_SCALE, s_new], axis=-1)
    prob = jax.nn.softmax(s, axis=-1).astype(ckv_past.dtype)
    return (jnp.einsum('bhqk,bkr->bqhr', prob[..., :P], ckv_past)
            + jnp.einsum('bhqk,bkr->bqhr', prob[..., P:], ckv_new))


def rwkv_mixer(cols, shift0, S0, p):
    B, T, _ = cols.shape
    f32 = jnp.float32
    prev = jnp.concatenate([shift0[:, None, :].astype(cols.dtype), cols[:, :-1]], axis=1)
    xs = cols + (prev - cols) * p['rwkv_mu']
    r, k, v, wd, ad, gd = split_cols(xs, RWKV_SPLITS)
    w_log = -jax.nn.softplus(-(p['rwkv_w0'] + jnp.tanh(wd) @ p['rwkv_w2'])) - 0.5
    decay = jnp.exp(-jnp.exp(w_log.astype(f32)))
    a = jax.nn.sigmoid(p['rwkv_a0'] + ad @ p['rwkv_a2'])
    g = jax.nn.sigmoid(gd) @ p['rwkv_g2']
    heads = lambda t: t.reshape(B, T, RWKV_HEADS, RWKV_HEAD).astype(f32)
    kk = heads(k * p['rwkv_k_k'])
    kk = kk / jnp.maximum(jnp.sqrt(jnp.sum(kk * kk, axis=-1, keepdims=True)), 1e-12)
    k = heads(k * (1.0 + (a - 1.0) * p['rwkv_k_a']))
    r, v, a, decay = heads(r), heads(v), heads(a), heads(decay)

    def step(S, inp):
        r_t, w_t, k_t, v_t, kk_t, a_t = inp
        sa = jnp.einsum('bhvk,bhk->bhv', S, -kk_t)
        S = S * w_t[:, :, None, :] + sa[..., None] * (kk_t * a_t)[:, :, None, :] + v_t[..., None] * k_t[:, :, None, :]
        return S, jnp.einsum('bhvk,bhk->bhv', S, r_t)

    tm = lambda t: jnp.moveaxis(t, 1, 0)
    S, y = lax.scan(step, S0.astype(f32), (tm(r), tm(decay), tm(k), tm(v), tm(kk), tm(a)))
    y = jnp.moveaxis(y, 0, 1)
    mean = jnp.mean(y, axis=-1, keepdims=True)
    var = jnp.mean(jnp.square(y - mean), axis=-1, keepdims=True)
    y = ((y - mean) * lax.rsqrt(var + RWKV_GN_EPS)).reshape(B, T, RWKV_W)
    y = y * p['rwkv_ln_w'].astype(f32) + p['rwkv_ln_b'].astype(f32)
    r_k = p['rwkv_r_k'].reshape(RWKV_HEADS, RWKV_HEAD).astype(f32)
    bonus = jnp.sum(r * k * r_k, axis=-1, keepdims=True) * v
    y = (y + bonus.reshape(B, T, RWKV_W)) * g.astype(f32)
    return y.astype(cols.dtype), S, cols[:, -1]


def stick_break_weights(z, valid, tail):
    log_fail = jnp.where(valid, jax.nn.log_sigmoid(-z), 0.0)
    rc = lax.cumsum(log_fail, axis=z.ndim - 1, reverse=True)
    after = jnp.concatenate([rc[..., 1:], jnp.zeros_like(rc[..., :1])], axis=-1)
    return jnp.where(valid, jnp.exp(jax.nn.log_sigmoid(z) + after + tail), 0.0)


def sb_prompt(q, k, v):
    kpos = jnp.arange(k.shape[1])
    def block(start, qb):
        z = jnp.einsum('bqhd,bkhd->bhqk', qb, k).astype(jnp.float32) * SB_SCALE
        qpos = start + jnp.arange(Q_BLOCK)
        w = stick_break_weights(z, kpos[None, :] < qpos[:, None], 0.0)
        return jnp.einsum('bhqk,bkhd->bqhd', w.astype(v.dtype), v)
    return sweep_query_blocks(block, q)


def sb_sample(q, k_new, v_new, k_past, v_past):
    Q = q.shape[1]
    z_past = jnp.einsum('bqhd,bkhd->bhqk', q, k_past).astype(jnp.float32) * SB_SCALE
    z_new = jnp.einsum('bqhd,bkhd->bhqk', q, k_new).astype(jnp.float32) * SB_SCALE
    valid_new = jnp.arange(Q)[None, :] < jnp.arange(Q)[:, None]
    tail = jnp.sum(jnp.where(valid_new, jax.nn.log_sigmoid(-z_new), 0.0), axis=-1, keepdims=True)
    w_past = stick_break_weights(z_past, True, tail)
    w_new = stick_break_weights(z_new, valid_new, 0.0)
    return (jnp.einsum('bhqk,bkhd->bqhd', w_past.astype(v_past.dtype), v_past)
            + jnp.einsum('bhqk,bkhd->bqhd', w_new.astype(v_new.dtype), v_new))


def gla_chunked(q, k, v, log_a, S0):
    B, T, H, K = q.shape
    C = math.gcd(T, GLA_CHUNK)
    n = T // C
    f32 = jnp.float32
    to_chunks = lambda t: jnp.moveaxis(t.astype(f32).reshape(B, n, C, *t.shape[2:]), 1, 0)
    causal = jnp.tril(jnp.ones((C, C), dtype=bool))

    def step(S, inp):
        qc, kc, vc, lac = inp
        b = jnp.cumsum(lac, axis=1)
        o_inter = jnp.einsum('bthk,bhkv->bthv', qc * jnp.exp(b), S)
        diff = jnp.where(causal[None, :, :, None, None], b[:, :, None] - b[:, None, :], -jnp.inf)
        att = jnp.einsum('bthk,bshk,btshk->bhts', qc, kc, jnp.exp(diff))
        o_intra = jnp.einsum('bhts,bshv->bthv', att, vc)
        b_last = b[:, -1]
        S = S * jnp.exp(b_last)[..., None] + jnp.einsum('bshk,bshv->bhkv', kc * jnp.exp(b_last[:, None] - b), vc)
        return S, o_inter + o_intra

    S, o = lax.scan(step, S0.astype(f32), (to_chunks(q), to_chunks(k), to_chunks(v), to_chunks(log_a)))
    o = jnp.moveaxis(o, 0, 1).reshape(B, T, H, v.shape[-1])
    return o.astype(v.dtype), S


def mixer_ab(h, pos, past, p):
    B, T, _ = h.shape
    cols = h @ p['w_in0']
    mla_cols, rwkv_cols = cols[..., :MLA_COLS], cols[..., MLA_COLS:]
    qa, kva, kpe = split_cols(mla_cols, MLA_SPLITS)
    q = (rmsnorm(qa, p['mla_q_norm']) @ p['mla_w_qb']).reshape(B, T, MLA_HEADS, MLA_NOPE + MLA_ROPE)
    q_pe = rope(q[..., MLA_NOPE:], pos)
    q_abs = jnp.einsum('bthd,rhd->bthr', q[..., :MLA_NOPE], p['mla_w_uk'])
    ckv = rmsnorm(kva, p['mla_kv_norm'])
    kpe = rope(kpe, pos)
    if past is None:
        lat = mla_prompt(q_abs, q_pe, ckv, kpe)
        S0 = jnp.zeros((B, RWKV_HEADS, RWKV_HEAD, RWKV_HEAD), jnp.float32)
        shift0 = jnp.zeros((B, RWKV_COLS), h.dtype)
    else:
        lat = mla_sample(q_abs, q_pe, ckv, kpe, past['mla_ckv'], past['mla_kpe'])
        S0, shift0 = past['rwkv_state'], past['rwkv_shift']
    a_out = jnp.einsum('bthr,rhd->bthd', lat, p['mla_w_uv']).reshape(B, T, MLA_HEADS * MLA_V)
    b_out, S, shift = rwkv_mixer(rwkv_cols, shift0, S0, p)
    y = jnp.concatenate([a_out, b_out], axis=-1) @ p['w_out0']
    return y, {'mla_ckv': ckv, 'mla_kpe': kpe, 'rwkv_state': S, 'rwkv_shift': shift}


def mixer_cd(h, past, p):
    B, T, _ = h.shape
    cols = h @ p['w_in1']
    sq, sk, sv = split_cols(cols[..., :SB_COLS], (SB_W, SB_W, SB_W))
    gq, gk, gv, ggd, gog = split_cols(cols[..., SB_COLS:], GLA_SPLITS)
    sbh = lambda t: t.reshape(B, T, SB_HEADS, SB_HEAD)
    sq, sk, sv = sbh(sq), sbh(sk), sbh(sv)
    if past is None:
        c_out = sb_prompt(sq, sk, sv)
        S0 = jnp.zeros((B, GLA_HEADS, GLA_DK, GLA_DV), jnp.float32)
    else:
        c_out = sb_sample(sq, sk, sv, past['sb_k'], past['sb_v'])
        S0 = past['gla_state']
    log_a = jax.nn.log_sigmoid((ggd @ p['gla_w_gup'] + p['gla_b_g']).astype(jnp.float32)) / GLA_GATE_NORM
    gh = lambda t, d: t.reshape(B, T, GLA_HEADS, d)
    o, S = gla_chunked(gh(gq, GLA_DK) * GLA_DK ** -0.5, gh(gk, GLA_DK), gh(gv, GLA_DV), gh(log_a, GLA_DK), S0)
    d_out = rmsnorm(o, p['gla_norm']).reshape(B, T, GLA_HEADS * GLA_DV) * jax.nn.silu(gog)
    y = jnp.concatenate([c_out.reshape(B, T, SB_W), d_out], axis=-1) @ p['w_out1']
    return y, {'sb_k': sk, 'sb_v': sv, 'gla_state': S}


def swiglu(h, w_gate, w_up, w_down):
    return (jax.nn.silu(h @ w_gate) * (h @ w_up)) @ w_down


def trunk(x, pos, past, p):
    new_state = {}
    for layer in range(DEPTH):
        h = rmsnorm(x, p['norm_mix'][layer])
        if layer % 2 == 0:
            y, st = mixer_ab(h, pos, past, p)
        else:
            y, st = mixer_cd(h, past, p)
        new_state.update(st)
        x = x + y
        h = rmsnorm(x, p['norm_ffn'][layer])
        x = x + swiglu(h, p['ffn_w_gate'][layer], p['ffn_w_up'][layer], p['ffn_w_down'][layer])
    return rmsnorm(x, p['norm_final']), new_state


def setup_inputs(seed: int = 0) -> dict:
    key = jax.random.key(seed)
    ks = iter(jax.random.split(key, 48))
    f32 = jnp.float32
    nrm = lambda shape, s: jax.random.normal(next(ks), shape, f32) * s
    gain = lambda shape: 1.0 + 0.01 * jax.random.normal(next(ks), shape, f32)
    n_pages = PAST_LEN // PAGE_SIZE
    n_used = DEC_BATCH * n_pages
    n_pool = (5 * n_used + 3) // 4
    inp = {}
    inp['x_prompt'] = nrm((BATCH, SEQ, D_MODEL), 1.0)
    inp['x_sample'] = nrm((DEC_BATCH, DEC_SEQ, D_MODEL), 1.0)
    inp['cache_mla_ckv'] = nrm((n_pool, PAGE_SIZE, MLA_KV_RANK), 1.0)
    inp['cache_mla_kpe'] = nrm((n_pool, PAGE_SIZE, MLA_ROPE), 1.0)
    inp['cache_sb_k'] = nrm((n_pool, PAGE_SIZE, SB_HEADS, SB_HEAD), 1.0)
    inp['cache_sb_v'] = nrm((n_pool, PAGE_SIZE, SB_HEADS, SB_HEAD), 1.0)
    inp['state_rwkv'] = nrm((DEC_BATCH, RWKV_HEADS, RWKV_HEAD, RWKV_HEAD), 1.0)
    inp['state_rwkv_shift'] = nrm((DEC_BATCH, RWKV_COLS), 1.0)
    inp['state_gla'] = nrm((DEC_BATCH, GLA_HEADS, GLA_DK, GLA_DV), 0.5)
    inp['page_table'] = jax.random.permutation(next(ks), n_pool)[:n_used].reshape(DEC_BATCH, n_pages).astype(jnp.int32)
    inp['w_in0'] = nrm((D_MODEL, IN0_COLS), D_MODEL ** -0.5)
    inp['mla_q_norm'] = gain((MLA_Q_RANK,))
    inp['mla_w_qb'] = nrm((MLA_Q_RANK, MLA_HEADS * (MLA_NOPE + MLA_ROPE)), MLA_Q_RANK ** -0.5)
    inp['mla_kv_norm'] = gain((MLA_KV_RANK,))
    inp['mla_w_uk'] = nrm((MLA_KV_RANK, MLA_HEADS, MLA_NOPE), MLA_KV_RANK ** -0.5)
    inp['mla_w_uv'] = nrm((MLA_KV_RANK, MLA_HEADS, MLA_V), MLA_KV_RANK ** -0.5)
    inp['rwkv_mu'] = jax.random.uniform(next(ks), (RWKV_COLS,), f32, 0.0, 1.0)
    inp['rwkv_w0'] = jax.random.uniform(next(ks), (RWKV_W,), f32, -6.0, 1.0)
    inp['rwkv_w2'] = nrm((RWKV_DECAY_RANK, RWKV_W), 0.1)
    inp['rwkv_a0'] = nrm((RWKV_W,), 0.1)
    inp['rwkv_a2'] = nrm((RWKV_A_RANK, RWKV_W), 0.1)
    inp['rwkv_g2'] = nrm((RWKV_GATE_RANK, RWKV_W), RWKV_GATE_RANK ** -0.5)
    inp['rwkv_k_k'] = 0.85 + nrm((RWKV_W,), 0.02)
    inp['rwkv_k_a'] = 1.0 + nrm((RWKV_W,), 0.02)
    inp['rwkv_r_k'] = nrm((RWKV_W,), 0.1)
    inp['rwkv_ln_w'] = gain((RWKV_W,))
    inp['rwkv_ln_b'] = nrm((RWKV_W,), 0.01)
    inp['w_out0'] = nrm((MIX0_W, D_MODEL), MIX0_W ** -0.5)
    inp['w_in1'] = nrm((D_MODEL, IN1_COLS), D_MODEL ** -0.5)
    inp['gla_w_gup'] = nrm((GLA_GATE_RANK, GLA_HEADS * GLA_DK), GLA_GATE_RANK ** -0.5)
    inp['gla_b_g'] = nrm((GLA_HEADS * GLA_DK,), 0.01)
    inp['gla_norm'] = gain((GLA_DV,))
    inp['w_out1'] = nrm((MIX1_W, D_MODEL), MIX1_W ** -0.5)
    inp['norm_mix'] = gain((DEPTH, D_MODEL))
    inp['norm_ffn'] = gain((DEPTH, D_MODEL))
    inp['ffn_w_gate'] = nrm((DEPTH, D_MODEL, D_FF), D_MODEL ** -0.5)
    inp['ffn_w_up'] = nrm((DEPTH, D_MODEL, D_FF), D_MODEL ** -0.5)
    inp['ffn_w_down'] = nrm((DEPTH, D_FF, D_MODEL), D_FF ** -0.5)
    inp['norm_final'] = gain((D_MODEL,))
    return inp


def reference(x_prompt, x_sample, cache_mla_ckv, cache_mla_kpe, cache_sb_k, cache_sb_v, state_rwkv,
              state_rwkv_shift, state_gla, page_table, w_in0, mla_q_norm, mla_w_qb, mla_kv_norm, mla_w_uk,
              mla_w_uv, rwkv_mu, rwkv_w0, rwkv_w2, rwkv_a0, rwkv_a2, rwkv_g2, rwkv_k_k, rwkv_k_a, rwkv_r_k,
              rwkv_ln_w, rwkv_ln_b, w_out0, w_in1, gla_w_gup, gla_b_g, gla_norm, w_out1, norm_mix, norm_ffn,
              ffn_w_gate, ffn_w_up, ffn_w_down, norm_final):
    p = {
        'w_in0': w_in0, 'mla_q_norm': mla_q_norm, 'mla_w_qb': mla_w_qb, 'mla_kv_norm': mla_kv_norm,
        'mla_w_uk': mla_w_uk, 'mla_w_uv': mla_w_uv, 'rwkv_mu': rwkv_mu, 'rwkv_w0': rwkv_w0,
        'rwkv_w2': rwkv_w2, 'rwkv_a0': rwkv_a0, 'rwkv_a2': rwkv_a2, 'rwkv_g2': rwkv_g2,
        'rwkv_k_k': rwkv_k_k, 'rwkv_k_a': rwkv_k_a, 'rwkv_r_k': rwkv_r_k, 'rwkv_ln_w': rwkv_ln_w,
        'rwkv_ln_b': rwkv_ln_b, 'w_out0': w_out0, 'w_in1': w_in1, 'gla_w_gup': gla_w_gup,
        'gla_b_g': gla_b_g, 'gla_norm': gla_norm, 'w_out1': w_out1, 'norm_mix': norm_mix,
        'norm_ffn': norm_ffn, 'ffn_w_gate': ffn_w_gate, 'ffn_w_up': ffn_w_up, 'ffn_w_down': ffn_w_down,
        'norm_final': norm_final,
    }
    pos_prompt = jnp.arange(x_prompt.shape[1], dtype=jnp.int32)
    pos_sample = PAST_LEN + jnp.arange(x_sample.shape[1], dtype=jnp.int32)
    y_prompt, sp = trunk(x_prompt, pos_prompt, None, p)
    past = {
        'mla_ckv': gather_pages(cache_mla_ckv, page_table),
        'mla_kpe': gather_pages(cache_mla_kpe, page_table),
        'sb_k': gather_pages(cache_sb_k, page_table),
        'sb_v': gather_pages(cache_sb_v, page_table),
        'rwkv_state': state_rwkv,
        'rwkv_shift': state_rwkv_shift,
        'gla_state': state_gla,
    }
    y_sample, ss = trunk(x_sample, pos_sample, past, p)
    return (y_prompt, y_sample,
            sp['mla_ckv'], sp['mla_kpe'], sp['rwkv_state'], sp['rwkv_shift'], sp['sb_k'], sp['sb_v'], sp['gla_state'],
            ss['mla_ckv'], ss['mla_kpe'], ss['rwkv_state'], ss['rwkv_shift'], ss['sb_k'], ss['sb_v'], ss['gla_state'])
```

```python
import functools
import math

import jax
import jax.numpy as jnp
from jax import lax
from jax.experimental import pallas as pl
from jax.experimental.pallas import tpu as pltpu

f32 = jnp.float32
bf16 = jnp.bfloat16

PAGE_SIZE = 128
MLA_HEADS = 8
MLA_NOPE = 128
MLA_ROPE = 64
MLA_V = 128
MLA_Q_RANK = 512
MLA_KV_RANK = 512
MLA_SCALE = (MLA_NOPE + MLA_ROPE) ** -0.5
ROPE_THETA = 10000.0
RWKV_HEADS = 16
RWKV_HEAD = 64
RWKV_W = RWKV_HEADS * RWKV_HEAD
RWKV_DECAY_RANK = 64
RWKV_A_RANK = 64
RWKV_GATE_RANK = 160
RWKV_GN_EPS = 64e-5
RWKV_COLS = 3 * RWKV_W + RWKV_DECAY_RANK + RWKV_A_RANK + RWKV_GATE_RANK
SB_HEADS = 16
SB_HEAD = 64
SB_W = SB_HEADS * SB_HEAD
SB_SCALE = SB_HEAD ** -0.5
GLA_HEADS = 4
GLA_DK = 128
GLA_DV = 256
GLA_GATE_RANK = 16
GLA_GATE_NORM = 16.0
NORM_EPS = 1e-6

LANES = 128
VMEM_LIMIT_BYTES = 56 * 1024 * 1024

C0_R, C0_K, C0_V = 0, 1024, 2048
C0_QA, C0_KVA = 3072, 3584
C0_KPE, C0_KPEROT = 4096, 4224
C0_WDAD = 4352
C0_GD = 4608
C0_N = 5120
C1_SQ, C1_SK, C1_SV = 0, 1024, 2048
C1_GQ, C1_GK, C1_GV = 3072, 3584, 4096
C1_GOG = 5120
C1_GGD = 6144
C1_N = 6656

RWKV_CHUNK = 64
GLA_CHUNK = 64
MLA_SLOT = 2 * LANES
ABS_SLOT = MLA_KV_RANK + LANES


def _cparams(*sem):
    return pltpu.CompilerParams(dimension_semantics=sem, vmem_limit_bytes=VMEM_LIMIT_BYTES)


def _dot(a, b):
    return jnp.dot(a, b, preferred_element_type=f32)


def _dot_nt(a, b):
    return lax.dot_general(a, b, (((1,), (1,)), ((), ())), preferred_element_type=f32)


def _dot_tn(a, b):
    return lax.dot_general(a, b, (((0,), (0,)), ((), ())), preferred_element_type=f32)


def _split_dot_right(x, m):
    hi = x.astype(bf16)
    lo = (x - hi.astype(f32)).astype(bf16)
    return _dot(hi, m) + _dot(lo, m)


def _split_dot_left(m, x):
    hi = x.astype(bf16)
    lo = (x - hi.astype(f32)).astype(bf16)
    return _dot(m, hi) + _dot(m, lo)


def _sigmoid(x):
    return 1.0 / (1.0 + jnp.exp(-x))


def _softplus(x):
    return jnp.maximum(x, 0.0) + jnp.log(1.0 + jnp.exp(-jnp.abs(x)))


def _rms(x, gain):
    return x * lax.rsqrt(jnp.mean(x * x, axis=-1, keepdims=True) + NORM_EPS) * gain


def _mm_body(*refs, n_lhs, norm, res):
    lhs = refs[:n_lhs]
    ws = refs[n_lhs:2 * n_lhs]
    pos = 2 * n_lhs
    g_ref = refs[pos] if norm else None
    pos += int(norm)
    r_ref = refs[pos] if res else None
    pos += int(res)
    o_ref = refs[pos]
    hs = refs[pos + 1:pos + 1 + n_lhs]

    @pl.when(pl.program_id(1) == 0)
    def _():
        for i in range(n_lhs):
            x = lhs[i][...].astype(f32)
            if norm and i == 0:
                x = _rms(x, g_ref[...])
            hs[i][...] = x.astype(bf16)

    acc = _dot(hs[0][...], ws[0][...])
    for i in range(1, n_lhs):
        acc = acc + _dot(hs[i][...], ws[i][...])
    if res:
        acc = acc + r_ref[...]
    o_ref[...] = acc.astype(o_ref.dtype)


def matmul(lhs_list, w_list, *, gain=None, res=None, out_dtype=f32, tm=512, tn=512, name):
    m = lhs_list[0][0].shape[0]
    n = w_list[0].shape[1]
    tm, tn = min(tm, m), min(tn, n)
    assert m % tm == 0 and n % tn == 0
    in_specs, args, scratch = [], [], []
    for arr, k, cb in lhs_list:
        in_specs.append(pl.BlockSpec((tm, k), lambda i, j, cb=cb: (i, cb)))
        args.append(arr)
        scratch.append(pltpu.VMEM((tm, k), bf16))
    for (arr, k, cb), w in zip(lhs_list, w_list):
        assert w.shape[0] == k
        in_specs.append(pl.BlockSpec((k, tn), lambda i, j: (0, j)))
        args.append(w)
    if gain is not None:
        in_specs.append(pl.BlockSpec((1, lhs_list[0][1]), lambda i, j: (0, 0)))
        args.append(gain.reshape(1, -1))
    if res is not None:
        in_specs.append(pl.BlockSpec((tm, tn), lambda i, j: (i, j)))
        args.append(res)
    body = functools.partial(_mm_body, n_lhs=len(lhs_list), norm=gain is not None, res=res is not None)
    return pl.pallas_call(
        body,
        out_shape=jax.ShapeDtypeStruct((m, n), out_dtype),
        grid=(m // tm, n // tn),
        in_specs=in_specs,
        out_specs=pl.BlockSpec((tm, tn), lambda i, j: (i, j)),
        scratch_shapes=scratch,
        compiler_params=_cparams("parallel", "arbitrary"),
        name=name,
    )(*args)


def _hmm_body(x_ref, w_ref, o_ref):
    o_ref[...] = _dot(x_ref[...].astype(bf16), w_ref[0]).astype(o_ref.dtype)


def head_matmul(x, w, *, name):
    m = x.shape[0]
    h, k, n = w.shape
    return pl.pallas_call(
        _hmm_body,
        out_shape=jax.ShapeDtypeStruct((m, h * n), f32),
        grid=(h,),
        in_specs=[pl.BlockSpec((m, k), lambda i: (0, i)), pl.BlockSpec((1, k, n), lambda i: (i, 0, 0))],
        out_specs=pl.BlockSpec((m, n), lambda i: (0, i)),
        compiler_params=_cparams("arbitrary"),
        name=name,
    )(x, w)


def _ffn_body(*refs, final):
    x_ref, g_ref, wg_ref, wu_ref, wd_ref = refs[:5]
    gf_ref = refs[5] if final else None
    o_ref, h_ref, acc_ref = refs[5 + int(final):]
    f = pl.program_id(1)

    @pl.when(f == 0)
    def _():
        h_ref[...] = _rms(x_ref[...], g_ref[...]).astype(bf16)
        acc_ref[...] = jnp.zeros_like(acc_ref)

    h = h_ref[...]
    a = _dot(h, wg_ref[...])
    u = _dot(h, wu_ref[...])
    s = (a * _sigmoid(a) * u).astype(bf16)
    acc_ref[...] += _dot(s, wd_ref[...])

    @pl.when(f == pl.num_programs(1) - 1)
    def _():
        y = x_ref[...] + acc_ref[...]
        if final:
            y = _rms(y, gf_ref[...])
        o_ref[...] = y


def ffn(x, gain, wg, wu, wd, *, final_gain=None, tm=512, tf=512, name):
    m, d = x.shape
    dff = wg.shape[1]
    tm = min(tm, m)
    assert m % tm == 0 and dff % tf == 0
    in_specs = [
        pl.BlockSpec((tm, d), lambda i, j: (i, 0)),
        pl.BlockSpec((1, d), lambda i, j: (0, 0)),
        pl.BlockSpec((d, tf), lambda i, j: (0, j)),
        pl.BlockSpec((d, tf), lambda i, j: (0, j)),
        pl.BlockSpec((tf, d), lambda i, j: (j, 0)),
    ]
    args = [x, gain.reshape(1, d), wg, wu, wd]
    if final_gain is not None:
        in_specs.append(pl.BlockSpec((1, d), lambda i, j: (0, 0)))
        args.append(final_gain.reshape(1, d))
    return pl.pallas_call(
        functools.partial(_ffn_body, final=final_gain is not None),
        out_shape=jax.ShapeDtypeStruct((m, d), f32),
        grid=(m // tm, dff // tf),
        in_specs=in_specs,
        out_specs=pl.BlockSpec((tm, d), lambda i, j: (i, 0)),
        scratch_shapes=[pltpu.VMEM((tm, d), bf16), pltpu.VMEM((tm, d), f32)],
        compiler_params=_cparams("parallel", "arbitrary"),
        name=name,
    )(*args)


def _mla_prep_body(*refs, absorbed):
    (qa_ref, kva_ref, kpe_ref, kperot_ref, cos_ref, sin_ref, qg_ref, wq_ref, kg_ref, wkv_ref) = refs[:10]
    outs = refs[10:]
    cos, sin = cos_ref[...], sin_ref[...]
    qn = _rms(qa_ref[...], qg_ref[...]).astype(bf16)
    qall = _dot(qn, wq_ref[...])
    hw = MLA_HEADS * LANES
    nope, pe, rot = qall[:, :hw], qall[:, hw:2 * hw], qall[:, 2 * hw:]
    cos8 = jnp.concatenate([cos] * MLA_HEADS, axis=1)
    sin8 = jnp.concatenate([sin] * MLA_HEADS, axis=1)
    roped = pe * cos8 + rot * sin8
    ckv = _rms(kva_ref[...], kg_ref[...])
    kpe = kpe_ref[...] * cos + kperot_ref[...] * sin
    if absorbed:
        qcat_ref, ckv_ref, kpe_out_ref, knew_ref = outs
        pieces = []
        for h in range(MLA_HEADS):
            qabs = _dot(nope[:, h * LANES:(h + 1) * LANES].astype(bf16), wkv_ref[h])
            pieces += [qabs, roped[:, h * LANES:(h + 1) * LANES]]
        qcat_ref[...] = jnp.concatenate(pieces, axis=1).astype(bf16)
        knew_ref[...] = jnp.concatenate([ckv, kpe], axis=1)
    else:
        qcat_ref, kcat_ref, v_ref, ckv_ref, kpe_out_ref = outs
        kv = _dot(ckv.astype(bf16), wkv_ref[...])
        qp, kp = [], []
        for h in range(MLA_HEADS):
            sl = slice(h * LANES, (h + 1) * LANES)
            qp += [nope[:, sl], roped[:, sl]]
            kp += [kv[:, sl], kpe]
        qcat_ref[...] = jnp.concatenate(qp, axis=1).astype(bf16)
        kcat_ref[...] = jnp.concatenate(kp, axis=1).astype(bf16)
        v_ref[...] = kv[:, hw:].astype(bf16)
    ckv_ref[...] = ckv
    kpe_out_ref[...] = kpe[:, :MLA_ROPE]


def mla_prep(cols0, cos, sin, q_gain, wq_all, kv_gain, wkv, *, absorbed, tm=256, name):
    m = cols0.shape[0]
    tm = min(tm, m)
    assert m % tm == 0
    row = lambda w, cb: pl.BlockSpec((tm, w), lambda i, cb=cb: (i, cb))
    full = lambda a: pl.BlockSpec(a.shape, lambda i, nd=a.ndim: (0,) * nd)
    qg, kg = q_gain.reshape(1, -1), kv_gain.reshape(1, -1)
    in_specs = [
        row(MLA_Q_RANK, C0_QA // MLA_Q_RANK), row(MLA_KV_RANK, C0_KVA // MLA_KV_RANK),
        row(LANES, C0_KPE // LANES), row(LANES, C0_KPEROT // LANES),
        row(LANES, 0), row(LANES, 0), full(qg), full(wq_all), full(kg), full(wkv),
    ]
    if absorbed:
        out_shape = [jax.ShapeDtypeStruct((m, MLA_HEADS * ABS_SLOT), bf16),
                     jax.ShapeDtypeStruct((m, MLA_KV_RANK), f32),
                     jax.ShapeDtypeStruct((m, MLA_ROPE), f32),
                     jax.ShapeDtypeStruct((m, ABS_SLOT), f32)]
        out_specs = [row(MLA_HEADS * ABS_SLOT, 0), row(MLA_KV_RANK, 0), row(MLA_ROPE, 0), row(ABS_SLOT, 0)]
    else:
        out_shape = [jax.ShapeDtypeStruct((m, MLA_HEADS * MLA_SLOT), bf16),
                     jax.ShapeDtypeStruct((m, MLA_HEADS * MLA_SLOT), bf16),
                     jax.ShapeDtypeStruct((m, MLA_HEADS * MLA_V), bf16),
                     jax.ShapeDtypeStruct((m, MLA_KV_RANK), f32),
                     jax.ShapeDtypeStruct((m, MLA_ROPE), f32)]
        out_specs = [row(MLA_HEADS * MLA_SLOT, 0), row(MLA_HEADS * MLA_SLOT, 0), row(MLA_HEADS * MLA_V, 0),
                     row(MLA_KV_RANK, 0), row(MLA_ROPE, 0)]
    return pl.pallas_call(
        functools.partial(_mla_prep_body, absorbed=absorbed),
        out_shape=out_shape,
        grid=(m // tm,),
        in_specs=in_specs,
        out_specs=out_specs,
        compiler_params=_cparams("parallel"),
        name=name,
    )(cols0, cols0, cols0, cols0, cos, sin, qg, wq_all, kg, wkv)


def _causal_pairs(nblk, reverse):
    qi, kj = [], []
    for i in range(nblk):
        ks = range(i, -1, -1) if reverse else range(i + 1)
        for j in ks:
            qi.append(i)
            kj.append(j)
    return jnp.asarray(qi, jnp.int32), jnp.asarray(kj, jnp.int32)


def _mla_flash_body(qi_ref, kj_ref, q_ref, k_ref, v_ref, o_ref, m_ref, l_ref, acc_ref, *, tq):
    p = pl.program_id(2)
    i, j = qi_ref[p], kj_ref[p]

    @pl.when(j == 0)
    def _():
        m_ref[...] = jnp.full_like(m_ref, -jnp.inf)
        l_ref[...] = jnp.zeros_like(l_ref)
        acc_ref[...] = jnp.zeros_like(acc_ref)

    s = _dot_nt(q_ref[...], k_ref[...]) * MLA_SCALE
    row = lax.broadcasted_iota(jnp.int32, s.shape, 0) + i * tq
    col = lax.broadcasted_iota(jnp.int32, s.shape, 1) + j * tq
    s = jnp.where(col <= row, s, -jnp.inf)
    m_prev = m_ref[...]
    m_new = jnp.maximum(m_prev, jnp.max(s, axis=-1, keepdims=True))
    alpha = jnp.exp(m_prev - m_new)
    pr = jnp.exp(s - m_new)
    l_ref[...] = l_ref[...] * alpha + jnp.sum(pr, axis=-1, keepdims=True)
    acc_ref[...] = acc_ref[...] * alpha + _dot(pr.astype(bf16), v_ref[...])
    m_ref[...] = m_new

    @pl.when(j == i)
    def _():
        o_ref[...] = acc_ref[...] / l_ref[...]


def mla_flash(qcat, kcat, v, batch, seq, *, tq=512, name):
    tq = min(tq, seq)
    nblk = seq // tq
    qi, kj = _causal_pairs(nblk, reverse=False)
    grid_spec = pltpu.PrefetchScalarGridSpec(
        num_scalar_prefetch=2,
        grid=(batch, MLA_HEADS, int(qi.shape[0])),
        in_specs=[
            pl.BlockSpec((tq, MLA_SLOT), lambda b, h, p, qi, kj: (b * nblk + qi[p], h)),
            pl.BlockSpec((tq, MLA_SLOT), lambda b, h, p, qi, kj: (b * nblk + kj[p], h)),
            pl.BlockSpec((tq, MLA_V), lambda b, h, p, qi, kj: (b * nblk + kj[p], h)),
        ],
        out_specs=pl.BlockSpec((tq, MLA_V), lambda b, h, p, qi, kj: (b * nblk + qi[p], h)),
        scratch_shapes=[pltpu.VMEM((tq, 1), f32), pltpu.VMEM((tq, 1), f32), pltpu.VMEM((tq, MLA_V), f32)],
    )
    return pl.pallas_call(
        functools.partial(_mla_flash_body, tq=tq),
        out_shape=jax.ShapeDtypeStruct((batch * seq, MLA_HEADS * MLA_V), f32),
        grid_spec=grid_spec,
        compiler_params=_cparams("parallel", "parallel", "arbitrary"),
        name=name,
    )(qi, kj, qcat, kcat, v)


def _sb_flash_body(qi_ref, kj_ref, q_ref, k_ref, v_ref, u_ref, o_ref, acc_ref, run_ref, *, tq, hp):
    p = pl.program_id(2)
    i, j = qi_ref[p], kj_ref[p]

    @pl.when(j == i)
    def _():
        acc_ref[...] = jnp.zeros_like(acc_ref)
        run_ref[...] = jnp.zeros_like(run_ref)

    q, k, v = q_ref[...].astype(bf16), k_ref[...].astype(bf16), v_ref[...].astype(bf16)
    upper = u_ref[...]
    for h in range(hp):
        sl = slice(h * SB_HEAD, (h + 1) * SB_HEAD)
        z = _dot_nt(q[:, sl], k[:, sl]) * SB_SCALE
        row = lax.broadcasted_iota(jnp.int32, z.shape, 0) + i * tq
        col = lax.broadcasted_iota(jnp.int32, z.shape, 1) + j * tq
        valid = col < row
        log_fail = jnp.where(valid, -_softplus(z), 0.0)
        after = _split_dot_right(log_fail, upper)
        run = run_ref[h]
        w = jnp.where(valid, jnp.exp(z + log_fail + after + run), 0.0)
        acc_ref[h] += _dot(w.astype(bf16), v[:, sl])
        run_ref[h] = run + jnp.sum(log_fail, axis=-1, keepdims=True)

    @pl.when(j == 0)
    def _():
        o_ref[...] = jnp.concatenate([acc_ref[h] for h in range(hp)], axis=1)


def sb_flash(cols1, batch, seq, *, tq=512, hp=2, name):
    tq = min(tq, seq)
    nblk = seq // tq
    qi, kj = _causal_pairs(nblk, reverse=True)
    w = hp * SB_HEAD
    nq, nk, nv = C1_SQ // w, C1_SK // w, C1_SV // w
    upper = (lax.broadcasted_iota(jnp.int32, (tq, tq), 0) > lax.broadcasted_iota(jnp.int32, (tq, tq), 1)).astype(bf16)
    grid_spec = pltpu.PrefetchScalarGridSpec(
        num_scalar_prefetch=2,
        grid=(batch, SB_HEADS // hp, int(qi.shape[0])),
        in_specs=[
            pl.BlockSpec((tq, w), lambda b, h, p, qi, kj: (b * nblk + qi[p], nq + h)),
            pl.BlockSpec((tq, w), lambda b, h, p, qi, kj: (b * nblk + kj[p], nk + h)),
            pl.BlockSpec((tq, w), lambda b, h, p, qi, kj: (b * nblk + kj[p], nv + h)),
            pl.BlockSpec((tq, tq), lambda b, h, p, qi, kj: (0, 0)),
        ],
        out_specs=pl.BlockSpec((tq, w), lambda b, h, p, qi, kj: (b * nblk + qi[p], h)),
        scratch_shapes=[pltpu.VMEM((hp, tq, SB_HEAD), f32), pltpu.VMEM((hp, tq, 1), f32)],
    )
    return pl.pallas_call(
        functools.partial(_sb_flash_body, tq=tq, hp=hp),
        out_shape=jax.ShapeDtypeStruct((batch * seq, SB_W), f32),
        grid_spec=grid_spec,
        compiler_params=_cparams("parallel", "parallel", "arbitrary"),
        name=name,
    )(qi, kj, cols1, cols1, cols1, upper)


def _rwkv_gates_body(*refs, seq_rows):
    (r_ref, k_ref, v_ref, wdad_ref, gd_ref) = refs[:5]
    pos = 5
    if seq_rows == 1:
        prevs = [ref[...] for ref in refs[pos:pos + 5]]
        pos += 5
    else:
        tails = refs[pos:pos + 5]
        firsts = refs[pos + 5:pos + 10]
        pos += 10
    (mu_r, mu_k, mu_v, mu_wdad, mu_gd, w0_ref, w2_ref, a0_ref, a2_ref, g2_ref, kk_ref, ka_ref, ones_ref) = refs[pos:pos + 13]
    (r_out, k_out, v_out, kkn_out, a_out, lw_out, g_out) = refs[pos + 13:]
    cur = [r_ref[...], k_ref[...], v_ref[...], wdad_ref[...], gd_ref[...]]
    if seq_rows != 1:
        tm = cur[0].shape[0]
        starts_seq = (pl.program_id(0) * tm) % seq_rows == 0
        prevs = []
        for x, tail, first in zip(cur, tails, firsts):
            carry = jnp.where(starts_seq, first[0], tail[7:8, :])
            rolled = pltpu.roll(x, 1, 0)
            rowid = lax.broadcasted_iota(jnp.int32, x.shape, 0)
            prevs.append(jnp.where(rowid == 0, carry, rolled))
    mus = [mu_r[...], mu_k[...], mu_v[...], mu_wdad[...], mu_gd[...]]
    xr, xk, xv, xwdad, xgd = [c + (p - c) * m for c, p, m in zip(cur, prevs, mus)]
    w_log = -_softplus(-(w0_ref[...] + _dot(jnp.tanh(xwdad).astype(bf16), w2_ref[...]))) - 0.5
    lw_out[...] = -jnp.exp(w_log)
    a = _sigmoid(a0_ref[...] + _dot(xwdad.astype(bf16), a2_ref[...]))
    g_out[...] = _dot(_sigmoid(xgd).astype(bf16), g2_ref[...])
    kk = xk * kk_ref[...]
    ssq = _split_dot_right(kk * kk, ones_ref[...])
    kkn_out[...] = kk / jnp.maximum(jnp.sqrt(ssq), 1e-12)
    k_out[...] = xk * (1.0 + (a - 1.0) * ka_ref[...])
    r_out[...] = xr
    v_out[...] = xv
    a_out[...] = a


def rwkv_gates(cols0, shift_p, seq_rows, prm, *, tm=256, name):
    m = cols0.shape[0]
    tm = min(tm, m, seq_rows) if seq_rows != 1 else min(tm, m)
    assert m % tm == 0 and (seq_rows == 1 or (seq_rows % tm == 0 and tm % 8 == 0))
    groups = [(RWKV_W, C0_R // RWKV_W), (RWKV_W, C0_K // RWKV_W), (RWKV_W, C0_V // RWKV_W),
              (LANES, C0_WDAD // LANES), (2 * LANES, C0_GD // (2 * LANES))]
    in_specs = [pl.BlockSpec((tm, w), lambda i, cb=cb: (i, cb)) for w, cb in groups]
    args = [cols0] * 5
    if seq_rows == 1:
        in_specs += [pl.BlockSpec((tm, w), lambda i, cb=cb: (i, cb)) for w, cb in groups]
        args += [shift_p] * 5
    else:
        per = tm // 8
        in_specs += [pl.BlockSpec((8, w), lambda i, cb=cb: (jnp.maximum(i * per - 1, 0), cb)) for w, cb in groups]
        args += [cols0] * 5
        shift3 = shift_p.reshape(shift_p.shape[0], 1, C0_N)
        in_specs += [pl.BlockSpec((1, 1, w), lambda i, cb=cb: ((i * tm) // seq_rows, 0, cb)) for w, cb in groups]
        args += [shift3] * 5
    small = [prm["mu_r"], prm["mu_k"], prm["mu_v"], prm["mu_wdad"], prm["mu_gd"], prm["w0"], prm["w2p"],
             prm["a0"], prm["a2p"], prm["g2p"], prm["k_k"], prm["k_a"], prm["head_ones"]]
    in_specs += [pl.BlockSpec(a.shape, lambda i: (0, 0)) for a in small]
    args += small
    out_spec = pl.BlockSpec((tm, RWKV_W), lambda i: (i, 0))
    return pl.pallas_call(
        functools.partial(_rwkv_gates_body, seq_rows=seq_rows),
        out_shape=[jax.ShapeDtypeStruct((m, RWKV_W), f32)] * 7,
        grid=(m // tm,),
        in_specs=in_specs,
        out_specs=[out_spec] * 7,
        compiler_params=_cparams("parallel"),
        name=name,
    )(*args)


def _rwkv_scan_body(r_ref, k_ref, v_ref, kk_ref, a_ref, lw_ref, g_ref, s0_ref, rk_ref, lnw_ref, lnb_ref,
                    y_ref, s_out_ref, s_ref):
    c = pl.program_id(1)
    C = r_ref.shape[0]

    @pl.when(c == 0)
    def _():
        s_ref[...] = s0_ref[0]

    rowi = lax.broadcasted_iota(jnp.int32, (C, C), 0)
    coli = lax.broadcasted_iota(jnp.int32, (C, C), 1)
    incl = coli <= rowi
    strict = coli < rowi
    lower = incl.astype(bf16)
    r, k, v, kk, a, lw, g = (x[...] for x in (r_ref, k_ref, v_ref, kk_ref, a_ref, lw_ref, g_ref))
    cum = _split_dot_left(lower, lw)
    gam = jnp.exp(cum)
    ginv = jnp.exp(-cum)
    a_m = -kk * jnp.exp(cum - lw)
    b_m = kk * a * ginv
    k_m = k * ginv
    r_m = r * gam
    cum_last = cum[C - 1:C, :]
    g_last = jnp.exp(cum_last)
    b_end = kk * a * jnp.exp(cum_last - cum)
    k_end = k * jnp.exp(cum_last - cum)
    bonus_w = r * k * rk_ref[...]
    lnw, lnb = lnw_ref[...], lnb_ref[...]
    pieces = []
    for h in range(RWKV_HEADS):
        sl = slice(h * RWKV_HEAD, (h + 1) * RWKV_HEAD)
        ar = jnp.concatenate([a_m[:, sl], r_m[:, sl]], axis=0).astype(bf16)
        bh, kh, vh = b_m[:, sl].astype(bf16), k_m[:, sl].astype(bf16), v[:, sl].astype(bf16)
        s = s_ref[h]
        g_b = _dot_nt(ar, bh)
        g_k = _dot_nt(ar, kh)
        g_s = _dot_nt(ar, s.astype(bf16))
        ab = jnp.where(strict, g_b[:C], 0.0)
        ak = jnp.where(strict, g_k[:C], 0.0)
        rb = jnp.where(incl, g_b[C:], 0.0)
        rkm = jnp.where(incl, g_k[C:], 0.0)
        x = g_s[:C] + _dot(ak.astype(bf16), vh)
        n = ab
        steps = int(math.log2(C))
        for it in range(steps):
            nb = n.astype(bf16)
            x = x + _dot(nb, x.astype(bf16))
            if it + 1 < steps:
                n = _dot(nb, nb)
        pv = jnp.concatenate([x, v[:, sl]], axis=0).astype(bf16)
        y = g_s[C:] + _dot(rb.astype(bf16), pv[:C]) + _dot(rkm.astype(bf16), vh)
        bk_end = jnp.concatenate([b_end[:, sl], k_end[:, sl]], axis=0).astype(bf16)
        s_ref[h] = s * g_last[:, sl] + _dot_tn(pv, bk_end)
        mean = jnp.mean(y, axis=-1, keepdims=True)
        var = jnp.mean(jnp.square(y - mean), axis=-1, keepdims=True)
        y = (y - mean) * lax.rsqrt(var + RWKV_GN_EPS) * lnw[:, sl] + lnb[:, sl]
        bonus = jnp.sum(bonus_w[:, sl], axis=-1, keepdims=True) * v[:, sl]
        pieces.append((y + bonus) * g[:, sl])
    y_ref[...] = jnp.concatenate(pieces, axis=1)

    @pl.when(c == pl.num_programs(1) - 1)
    def _():
        s_out_ref[0] = s_ref[...]


def rwkv_scan(gates, s0, prm, batch, seq, *, name):
    C = min(RWKV_CHUNK, seq)
    nch = seq // C
    tok = pl.BlockSpec((C, RWKV_W), lambda b, c: (b * nch + c, 0))
    st = pl.BlockSpec((1, RWKV_HEADS, RWKV_HEAD, RWKV_HEAD), lambda b, c: (b, 0, 0, 0))
    vec = pl.BlockSpec((1, RWKV_W), lambda b, c: (0, 0))
    r, k, v, kkn, a, lw, g = gates
    return pl.pallas_call(
        _rwkv_scan_body,
        out_shape=[jax.ShapeDtypeStruct((batch * seq, RWKV_W), f32),
                   jax.ShapeDtypeStruct((batch, RWKV_HEADS, RWKV_HEAD, RWKV_HEAD), f32)],
        grid=(batch, nch),
        in_specs=[tok] * 7 + [st, vec, vec, vec],
        out_specs=[tok, st],
        scratch_shapes=[pltpu.VMEM((RWKV_HEADS, RWKV_HEAD, RWKV_HEAD), f32)],
        compiler_params=_cparams("parallel", "arbitrary"),
        name=name,
    )(r, k, v, kkn, a, lw, g, s0, prm["r_k"], prm["ln_w"], prm["ln_b"])


def _rwkv_step_body(r_ref, k_ref, v_ref, kk_ref, a_ref, lw_ref, g_ref, s0_ref, rk_ref, lnw_ref, lnb_ref,
                    y_ref, s_out_ref):
    n = RWKV_HEAD
    eye = (lax.broadcasted_iota(jnp.int32, (n, n), 0) == lax.broadcasted_iota(jnp.int32, (n, n), 1)).astype(f32)
    r, k, v, kk, a, lw, g = (x[...] for x in (r_ref, k_ref, v_ref, kk_ref, a_ref, lw_ref, g_ref))
    s = s0_ref[...]
    sa = jnp.sum(s * (-kk), axis=-1, keepdims=True)
    vcol = jnp.sum(eye * v, axis=-1, keepdims=True)
    s = s * jnp.exp(lw) + sa * (kk * a) + vcol * k
    s_out_ref[...] = s
    ycol = jnp.sum(s * r, axis=-1, keepdims=True)
    y = jnp.sum(eye * ycol, axis=-2, keepdims=True)
    mean = jnp.mean(y, axis=-1, keepdims=True)
    var = jnp.mean(jnp.square(y - mean), axis=-1, keepdims=True)
    y = (y - mean) * lax.rsqrt(var + RWKV_GN_EPS) * lnw_ref[...] + lnb_ref[...]
    bonus = jnp.sum(r * k * rk_ref[...], axis=-1, keepdims=True) * v
    y_ref[...] = (y + bonus) * g


def rwkv_step(gates, s0, prm, *, bs=8, name):
    m = s0.shape[0]
    bs = min(bs, m)
    assert m % bs == 0
    h4 = lambda x: x.reshape(-1, RWKV_HEADS, 1, RWKV_HEAD)
    vec = pl.BlockSpec((bs, RWKV_HEADS, 1, RWKV_HEAD), lambda i: (i, 0, 0, 0))
    st = pl.BlockSpec((bs, RWKV_HEADS, RWKV_HEAD, RWKV_HEAD), lambda i: (i, 0, 0, 0))
    par = pl.BlockSpec((1, RWKV_HEADS, 1, RWKV_HEAD), lambda i: (0, 0, 0, 0))
    y, s = pl.pallas_call(
        _rwkv_step_body,
        out_shape=[jax.ShapeDtypeStruct((m, RWKV_HEADS, 1, RWKV_HEAD), f32), jax.ShapeDtypeStruct(s0.shape, f32)],
        grid=(m // bs,),
        in_specs=[vec] * 7 + [st, par, par, par],
        out_specs=[vec, st],
        compiler_params=_cparams("parallel"),
        name=name,
    )(*[h4(x) for x in gates], s0, h4(prm["r_k"]), h4(prm["ln_w"]), h4(prm["ln_b"]))
    return y.reshape(m, RWKV_W), s


def _gla_log_gate(ggd, wgup, bg):
    x = _dot(ggd.astype(bf16), wgup) + bg
    return -_softplus(-x) / GLA_GATE_NORM


def _gla_out(o, gain, gog):
    return _rms(o, gain) * (gog * _sigmoid(gog))


def _gla_scan_body(q_ref, k_ref, v_ref, ggd_ref, gog_ref, s0_ref, wgup_ref, bg_ref, gn_ref,
                   d_ref, s_out_ref, s_ref):
    c = pl.program_id(1)
    C = q_ref.shape[0]

    @pl.when(c == 0)
    def _():
        s_ref[...] = s0_ref[0]

    rowi = lax.broadcasted_iota(jnp.int32, (C, C), 0)
    coli = lax.broadcasted_iota(jnp.int32, (C, C), 1)
    incl = coli <= rowi
    lower = incl.astype(bf16)
    eye = (lax.broadcasted_iota(jnp.int32, (GLA_DK, GLA_DK), 0)
           == lax.broadcasted_iota(jnp.int32, (GLA_DK, GLA_DK), 1)).astype(f32)
    la = _gla_log_gate(ggd_ref[...], wgup_ref[...], bg_ref[...])
    b = _split_dot_left(lower, la)
    mid = (C // 2) // 8 * 8
    b_mid = b[mid:mid + 1, :]
    b_last = b[C - 1:C, :]
    qs = q_ref[...] * GLA_DK ** -0.5
    k, v, gog = k_ref[...], v_ref[...], gog_ref[...]
    q_inter = (qs * jnp.exp(b)).astype(bf16)
    q_mid = (qs * jnp.exp(b - b_mid)).astype(bf16)
    k_mid = (k * jnp.exp(b_mid - b)).astype(bf16)
    k_end = (k * jnp.exp(b_last - b)).astype(bf16)
    e_last = jnp.exp(b_last)
    pieces = []
    for h in range(GLA_HEADS):
        ks = slice(h * GLA_DK, (h + 1) * GLA_DK)
        vs = slice(h * GLA_DV, (h + 1) * GLA_DV)
        s = s_ref[h]
        vh = v[:, vs].astype(bf16)
        att = jnp.where(incl, _dot_nt(q_mid[:, ks], k_mid[:, ks]), 0.0)
        o = _dot(q_inter[:, ks], s.astype(bf16)) + _dot(att.astype(bf16), vh)
        e_col = jnp.sum(eye * e_last[:, ks], axis=-1, keepdims=True)
        s_ref[h] = s * e_col + _dot_tn(k_end[:, ks], vh)
        pieces.append(_gla_out(o, gn_ref[...], gog[:, vs]))
    d_ref[...] = jnp.concatenate(pieces, axis=1)

    @pl.when(c == pl.num_programs(1) - 1)
    def _():
        s_out_ref[0] = s_ref[...]


def gla_scan(cols1, s0, prm, batch, seq, *, name):
    C = min(GLA_CHUNK, seq)
    nch = seq // C
    kw, vw = GLA_HEADS * GLA_DK, GLA_HEADS * GLA_DV
    tok = lambda w, off: pl.BlockSpec((C, w), lambda b, c, cb=off // w: (b * nch + c, cb))
    st = pl.BlockSpec((1, GLA_HEADS, GLA_DK, GLA_DV), lambda b, c: (b, 0, 0, 0))
    full = lambda a: pl.BlockSpec(a.shape, lambda b, c: (0, 0))
    return pl.pallas_call(
        _gla_scan_body,
        out_shape=[jax.ShapeDtypeStruct((batch * seq, vw), f32),
                   jax.ShapeDtypeStruct((batch, GLA_HEADS, GLA_DK, GLA_DV), f32)],
        grid=(batch, nch),
        in_specs=[tok(kw, C1_GQ), tok(kw, C1_GK), tok(vw, C1_GV), tok(LANES, C1_GGD), tok(vw, C1_GOG), st,
                  full(prm["wgup"]), full(prm["bg"]), full(prm["gn"])],
        out_specs=[pl.BlockSpec((C, vw), lambda b, c: (b * nch + c, 0)), st],
        scratch_shapes=[pltpu.VMEM((GLA_HEADS, GLA_DK, GLA_DV), f32)],
        compiler_params=_cparams("parallel", "arbitrary"),
        name=name,
    )(cols1, cols1, cols1, cols1, cols1, s0, prm["wgup"], prm["bg"], prm["gn"])


def _gla_gate_body(ggd_ref, wgup_ref, bg_ref, la_ref):
    la_ref[...] = _gla_log_gate(ggd_ref[...], wgup_ref[...], bg_ref[...])


def gla_gate(cols1, prm, *, name):
    m = cols1.shape[0]
    kw = GLA_HEADS * GLA_DK
    return pl.pallas_call(
        _gla_gate_body,
        out_shape=jax.ShapeDtypeStruct((m, kw), f32),
        grid=(1,),
        in_specs=[pl.BlockSpec((m, LANES), lambda i: (0, C1_GGD // LANES)),
                  pl.BlockSpec(prm["wgup"].shape, lambda i: (0, 0)), pl.BlockSpec(prm["bg"].shape, lambda i: (0, 0))],
        out_specs=pl.BlockSpec((m, kw), lambda i: (0, 0)),
        compiler_params=_cparams("arbitrary"),
        name=name,
    )(cols1, prm["wgup"], prm["bg"])


def _gla_step_body(q_ref, k_ref, la_ref, v_ref, gog_ref, s0_ref, gn_ref, d_ref, s_out_ref):
    n = GLA_DK
    eye = (lax.broadcasted_iota(jnp.int32, (n, n), 0) == lax.broadcasted_iota(jnp.int32, (n, n), 1)).astype(f32)
    col = lambda x: jnp.sum(eye * x, axis=-1, keepdims=True)
    qc = col(q_ref[...] * GLA_DK ** -0.5)
    kc = col(k_ref[...])
    ec = col(jnp.exp(la_ref[...]))
    s = s0_ref[...] * ec + kc * v_ref[...]
    s_out_ref[...] = s
    o = jnp.sum(qc * s, axis=-2, keepdims=True)
    d_ref[...] = _gla_out(o, gn_ref[...], gog_ref[...])


def gla_step(q, k, la, v, gog, s0, prm, *, bs=8, name):
    m = s0.shape[0]
    bs = min(bs, m)
    assert m % bs == 0
    hk = lambda x: x.reshape(m, GLA_HEADS, 1, GLA_DK)
    hv = lambda x: x.reshape(m, GLA_HEADS, 1, GLA_DV)
    ks = pl.BlockSpec((bs, GLA_HEADS, 1, GLA_DK), lambda i: (i, 0, 0, 0))
    vs = pl.BlockSpec((bs, GLA_HEADS, 1, GLA_DV), lambda i: (i, 0, 0, 0))
    st = pl.BlockSpec((bs, GLA_HEADS, GLA_DK, GLA_DV), lambda i: (i, 0, 0, 0))
    gn = prm["gn"].reshape(1, 1, 1, GLA_DV)
    d, s = pl.pallas_call(
        _gla_step_body,
        out_shape=[jax.ShapeDtypeStruct((m, GLA_HEADS, 1, GLA_DV), f32), jax.ShapeDtypeStruct(s0.shape, f32)],
        grid=(m // bs,),
        in_specs=[ks, ks, ks, vs, vs, st, pl.BlockSpec(gn.shape, lambda i: (0, 0, 0, 0))],
        out_specs=[vs, st],
        compiler_params=_cparams("parallel"),
        name=name,
    )(hk(q), hk(k), hk(la), hv(v), hv(gog), s0, gn)
    return d.reshape(m, GLA_HEADS * GLA_DV), s


def _mla_decode_body(*refs, pages):
    pt_ref, q_ref, knew_ref = refs[:3]
    ckv_refs = refs[3:3 + pages]
    kpe_refs = refs[3 + pages:3 + 2 * pages]
    o_ref, m_ref, l_ref, acc_ref = refs[3 + 2 * pages:]
    j = pl.program_id(1)
    q = q_ref[0]
    q_abs, q_pe = q[:, :MLA_KV_RANK], q[:, MLA_KV_RANK:MLA_KV_RANK + MLA_ROPE]

    @pl.when(j == 0)
    def _():
        kn = knew_ref[0].astype(bf16).astype(f32)
        s_new = jnp.sum(q.astype(f32) * kn, axis=-1, keepdims=True) * MLA_SCALE
        m_ref[...] = s_new
        l_ref[...] = jnp.ones_like(l_ref)
        acc_ref[...] = jnp.broadcast_to(kn[:, :MLA_KV_RANK], acc_ref.shape)

    cks = [r[0].astype(bf16) for r in ckv_refs]
    scores = [(_dot_nt(q_abs, ck) + _dot_nt(q_pe, kr[0].astype(bf16))) * MLA_SCALE
              for ck, kr in zip(cks, kpe_refs)]
    s = jnp.concatenate(scores, axis=1)
    m_prev = m_ref[...]
    m_new = jnp.maximum(m_prev, jnp.max(s, axis=-1, keepdims=True))
    alpha = jnp.exp(m_prev - m_new)
    pr = jnp.exp(s - m_new)
    l_ref[...] = l_ref[...] * alpha + jnp.sum(pr, axis=-1, keepdims=True)
    acc = acc_ref[...] * alpha
    for i, ck in enumerate(cks):
        acc = acc + _dot(pr[:, i * PAGE_SIZE:(i + 1) * PAGE_SIZE].astype(bf16), ck)
    acc_ref[...] = acc
    m_ref[...] = m_new

    @pl.when(j == pl.num_programs(1) - 1)
    def _():
        o_ref[0] = acc_ref[...] / l_ref[...]


def mla_decode(qcat, knew, cache_ckv, cache_kpe, page_table, *, pages=16, name):
    nseq, npages = page_table.shape
    pages = min(pages, npages)
    assert npages % pages == 0
    q3 = qcat.reshape(nseq, MLA_HEADS, ABS_SLOT)
    kn3 = knew.reshape(nseq, 1, ABS_SLOT)
    page = lambda w, i: pl.BlockSpec((1, PAGE_SIZE, w), lambda b, j, pt, i=i: (pt[b, j * pages + i], 0, 0))
    grid_spec = pltpu.PrefetchScalarGridSpec(
        num_scalar_prefetch=1,
        grid=(nseq, npages // pages),
        in_specs=[pl.BlockSpec((1, MLA_HEADS, ABS_SLOT), lambda b, j, pt: (b, 0, 0)),
                  pl.BlockSpec((1, 1, ABS_SLOT), lambda b, j, pt: (b, 0, 0))]
        + [page(MLA_KV_RANK, i) for i in range(pages)] + [page(MLA_ROPE, i) for i in range(pages)],
        out_specs=pl.BlockSpec((1, MLA_HEADS, MLA_KV_RANK), lambda b, j, pt: (b, 0, 0)),
        scratch_shapes=[pltpu.VMEM((MLA_HEADS, 1), f32), pltpu.VMEM((MLA_HEADS, 1), f32),
                        pltpu.VMEM((MLA_HEADS, MLA_KV_RANK), f32)],
    )
    lat = pl.pallas_call(
        functools.partial(_mla_decode_body, pages=pages),
        out_shape=jax.ShapeDtypeStruct((nseq, MLA_HEADS, MLA_KV_RANK), f32),
        grid_spec=grid_spec,
        compiler_params=_cparams("parallel", "arbitrary"),
        name=name,
    )(page_table, q3, kn3, *([cache_ckv] * pages), *([cache_kpe] * pages))
    return lat.reshape(nseq, MLA_HEADS * MLA_KV_RANK)


def _sb_decode_body(*refs, pages):
    pt_ref, qbd_ref, expand_ref, upper_ref = refs[:4]
    k_refs = refs[4:4 + pages]
    v_refs = refs[4 + pages:4 + 2 * pages]
    o_ref, acc_ref, run_ref = refs[4 + 2 * pages:]
    j = pl.program_id(1)

    @pl.when(j == 0)
    def _():
        acc_ref[...] = jnp.zeros_like(acc_ref)
        run_ref[...] = jnp.zeros_like(run_ref)

    qbd = qbd_ref[0]
    expand = expand_ref[...]
    upper = upper_ref[...]
    acc = acc_ref[...]
    run = run_ref[...]
    for k_ref, v_ref in zip(k_refs, v_refs):
        z = _dot(k_ref[0].astype(bf16), qbd) * SB_SCALE
        log_fail = -_softplus(z)
        after = _split_dot_left(upper, log_fail)
        w = jnp.exp(z + log_fail + after + run)
        run = run + jnp.sum(log_fail, axis=0, keepdims=True)
        wv = _dot(w.astype(bf16), expand) * v_ref[0]
        acc = acc + jnp.sum(wv.reshape(PAGE_SIZE // 8, 8, SB_W), axis=0)
    acc_ref[...] = acc
    run_ref[...] = run

    @pl.when(j == pl.num_programs(1) - 1)
    def _():
        o_ref[0] = jnp.sum(acc_ref[...], axis=0, keepdims=True)


def sb_decode(q, cache_k, cache_v, page_table, *, pages=8, name):
    nseq, npages = page_table.shape
    pages = min(pages, npages)
    assert npages % pages == 0
    npool = cache_k.shape[0]
    ck = cache_k.reshape(npool, PAGE_SIZE, SB_W)
    cv = cache_v.reshape(npool, PAGE_SIZE, SB_W)
    head_of = jnp.arange(SB_W, dtype=jnp.int32) // SB_HEAD
    sel = head_of[:, None] == jnp.arange(LANES, dtype=jnp.int32)[None, :]
    qbd = jnp.where(sel[None], q[:, :, None], 0.0).astype(bf16)
    expand = sel.T.astype(bf16)
    upper = (lax.broadcasted_iota(jnp.int32, (PAGE_SIZE, PAGE_SIZE), 1)
             > lax.broadcasted_iota(jnp.int32, (PAGE_SIZE, PAGE_SIZE), 0)).astype(bf16)
    page = lambda i: pl.BlockSpec(
        (1, PAGE_SIZE, SB_W), lambda b, j, pt, i=i: (pt[b, npages - 1 - (j * pages + i)], 0, 0))
    grid_spec = pltpu.PrefetchScalarGridSpec(
        num_scalar_prefetch=1,
        grid=(nseq, npages // pages),
        in_specs=[pl.BlockSpec((1, SB_W, LANES), lambda b, j, pt: (b, 0, 0)),
                  pl.BlockSpec((LANES, SB_W), lambda b, j, pt: (0, 0)),
                  pl.BlockSpec((PAGE_SIZE, PAGE_SIZE), lambda b, j, pt: (0, 0))]
        + [page(i) for i in range(pages)] * 2,
        out_specs=pl.BlockSpec((1, 1, SB_W), lambda b, j, pt: (b, 0, 0)),
        scratch_shapes=[pltpu.VMEM((8, SB_W), f32), pltpu.VMEM((1, LANES), f32)],
    )
    out = pl.pallas_call(
        functools.partial(_sb_decode_body, pages=pages),
        out_shape=jax.ShapeDtypeStruct((nseq, 1, SB_W), f32),
        grid_spec=grid_spec,
        compiler_params=_cparams("parallel", "arbitrary"),
        name=name,
    )(page_table, qbd, expand, upper, *([ck] * pages), *([cv] * pages))
    return out.reshape(nseq, SB_W)


def _place(width, pieces, dtype=None):
    lead = pieces[0][1].shape[:-1]
    out, pos = [], 0
    for off, arr in pieces:
        if off > pos:
            out.append(jnp.zeros(lead + (off - pos,), arr.dtype))
        out.append(arr)
        pos = off + arr.shape[-1]
    if width > pos:
        out.append(jnp.zeros(lead + (width - pos,), pieces[0][1].dtype))
    res = jnp.concatenate(out, axis=-1)
    return res if dtype is None else res.astype(dtype)


def _rot_half_cols(w):
    half = w.shape[-1] // 2
    return jnp.concatenate([-w[..., half:], w[..., :half]], axis=-1)


def _rwkv_col_pieces(x):
    w = RWKV_W
    return [(C0_R, x[..., :3 * w]), (C0_WDAD, x[..., 3 * w:3 * w + 128]), (C0_GD, x[..., 3 * w + 128:])]


def _prepare(p):
    out = {}
    mla_cols = 2 * MLA_Q_RANK + MLA_ROPE
    w_in0 = p["w_in0"]
    w_kpe = w_in0[:, 2 * MLA_Q_RANK:mla_cols]
    out["w_in0"] = _place(C0_N, _rwkv_col_pieces(w_in0[:, mla_cols:])[:1] + [
        (C0_QA, w_in0[:, :2 * MLA_Q_RANK]), (C0_KPE, w_kpe), (C0_KPEROT, _rot_half_cols(w_kpe)),
    ] + _rwkv_col_pieces(w_in0[:, mla_cols:])[1:], bf16)
    d_qk = MLA_NOPE + MLA_ROPE
    wqb = p["mla_w_qb"].reshape(MLA_Q_RANK, MLA_HEADS, d_qk)
    nope = wqb[:, :, :MLA_NOPE]
    rope = wqb[:, :, MLA_NOPE:]
    pad = jnp.zeros((MLA_Q_RANK, MLA_HEADS, LANES - MLA_ROPE), f32)
    flat = lambda x: x.reshape(MLA_Q_RANK, -1)
    out["wq_all"] = jnp.concatenate(
        [flat(nope), flat(jnp.concatenate([rope, pad], -1)), flat(jnp.concatenate([_rot_half_cols(rope), pad], -1))],
        axis=1).astype(bf16)
    out["w_kv"] = jnp.concatenate([p["mla_w_uk"].reshape(MLA_KV_RANK, -1), p["mla_w_uv"].reshape(MLA_KV_RANK, -1)],
                                  axis=1).astype(bf16)
    out["w_uk_t"] = jnp.transpose(p["mla_w_uk"], (1, 2, 0)).astype(bf16)
    out["w_uv_h"] = jnp.transpose(p["mla_w_uv"], (1, 0, 2)).astype(bf16)
    mu = _place(C0_N, _rwkv_col_pieces(p["rwkv_mu"].reshape(1, -1)))
    zrows = lambda n: jnp.zeros((n, RWKV_W), f32)
    head_id = jnp.arange(RWKV_W, dtype=jnp.int32) // RWKV_HEAD
    out["rwkv"] = {
        "mu_r": mu[:, C0_R:C0_R + RWKV_W], "mu_k": mu[:, C0_K:C0_K + RWKV_W], "mu_v": mu[:, C0_V:C0_V + RWKV_W],
        "mu_wdad": mu[:, C0_WDAD:C0_WDAD + LANES], "mu_gd": mu[:, C0_GD:C0_GD + 2 * LANES],
        "w0": p["rwkv_w0"].reshape(1, -1), "a0": p["rwkv_a0"].reshape(1, -1),
        "w2p": jnp.concatenate([p["rwkv_w2"], zrows(LANES - RWKV_DECAY_RANK)], 0).astype(bf16),
        "a2p": jnp.concatenate([zrows(RWKV_DECAY_RANK), p["rwkv_a2"]], 0).astype(bf16),
        "g2p": jnp.concatenate([p["rwkv_g2"], zrows(2 * LANES - RWKV_GATE_RANK)], 0).astype(bf16),
        "k_k": p["rwkv_k_k"].reshape(1, -1), "k_a": p["rwkv_k_a"].reshape(1, -1),
        "head_ones": (head_id[:, None] == head_id[None, :]).astype(bf16),
        "r_k": p["rwkv_r_k"].reshape(1, -1), "ln_w": p["rwkv_ln_w"].reshape(1, -1), "ln_b": p["rwkv_ln_b"].reshape(1, -1),
    }
    w_in1 = p["w_in1"]
    ggd0 = 3 * SB_W + 2 * GLA_HEADS * GLA_DK + GLA_HEADS * GLA_DV
    out["w_in1"] = _place(C1_N, [(0, w_in1[:, :ggd0]), (C1_GOG, w_in1[:, ggd0 + GLA_GATE_RANK:]),
                                 (C1_GGD, w_in1[:, ggd0:ggd0 + GLA_GATE_RANK])], bf16)
    out["gla"] = {
        "wgup": jnp.concatenate([p["gla_w_gup"], jnp.zeros((LANES - GLA_GATE_RANK, GLA_HEADS * GLA_DK), f32)],
                                0).astype(bf16),
        "bg": p["gla_b_g"].reshape(1, -1), "gn": p["gla_norm"].reshape(1, -1),
    }
    for name in ("w_out0", "w_out1", "ffn_w_gate", "ffn_w_up", "ffn_w_down"):
        out[name] = p[name].astype(bf16)
    return out


def _rope_tables(pos):
    half = MLA_ROPE // 2
    inv = 1.0 / (ROPE_THETA ** (jnp.arange(half, dtype=f32) / half))
    ang = pos.astype(f32)[:, None] * inv[None, :]
    cos, sin = jnp.cos(ang), jnp.sin(ang)
    return jnp.concatenate([cos] * 4, axis=1), jnp.concatenate([sin] * 4, axis=1)


def _rwkv_shift_row(cols0_row):
    return jnp.concatenate([cols0_row[..., C0_R:C0_R + 3 * RWKV_W], cols0_row[..., C0_WDAD:C0_WDAD + LANES],
                            cols0_row[..., C0_GD:C0_GD + RWKV_GATE_RANK]], axis=-1)


def _trunk(x, pos, past, p, w, batch, seq, tag):
    m = batch * seq
    cos, sin = _rope_tables(pos)
    cos, sin = jnp.tile(cos, (batch, 1)), jnp.tile(sin, (batch, 1))
    state = {}
    tmm = 1024 if m >= 1024 else m

    cols0 = matmul([(x, x.shape[1], 0)], [w["w_in0"]], gain=p["norm_mix"][0], tm=tmm, name=f"{tag}_in0")
    if past is None:
        qcat, kcat, v, ckv, kpe = mla_prep(cols0, cos, sin, p["mla_q_norm"], w["wq_all"], p["mla_kv_norm"],
                                           w["w_kv"], absorbed=False, name=f"{tag}_mla_prep")
        a_out = mla_flash(qcat, kcat, v, batch, seq, name=f"{tag}_mla_attn")
        shift_p = jnp.zeros((batch, C0_N), f32)
        s0 = jnp.zeros((batch, RWKV_HEADS, RWKV_HEAD, RWKV_HEAD), f32)
        gates = rwkv_gates(cols0, shift_p, seq, w["rwkv"], name=f"{tag}_rwkv_gates")
        b_out, s_rwkv = rwkv_scan(gates, s0, w["rwkv"], batch, seq, name=f"{tag}_rwkv_scan")
    else:
        qcat, ckv, kpe, knew = mla_prep(cols0, cos, sin, p["mla_q_norm"], w["wq_all"], p["mla_kv_norm"],
                                        w["w_uk_t"], absorbed=True, name=f"{tag}_mla_prep")
        lat = mla_decode(qcat, knew, past["mla_ckv"], past["mla_kpe"], past["page_table"], name=f"{tag}_mla_attn")
        a_out = head_matmul(lat, w["w_uv_h"], name=f"{tag}_mla_uv")
        shift_p = _place(C0_N, _rwkv_col_pieces(past["rwkv_shift"]))
        gates = rwkv_gates(cols0, shift_p, 1, w["rwkv"], name=f"{tag}_rwkv_gates")
        b_out, s_rwkv = rwkv_step(gates, past["rwkv_state"], w["rwkv"], name=f"{tag}_rwkv_step")
    state.update(mla_ckv=ckv.reshape(batch, seq, MLA_KV_RANK), mla_kpe=kpe.reshape(batch, seq, MLA_ROPE),
                 rwkv_state=s_rwkv, rwkv_shift=_rwkv_shift_row(cols0.reshape(batch, seq, C0_N)[:, -1]))
    half = w["w_out0"].shape[0] // 2
    x = matmul([(a_out, half, 0), (b_out, half, 0)], [w["w_out0"][:half], w["w_out0"][half:]], res=x, tm=tmm,
               name=f"{tag}_out0")
    x = ffn(x, p["norm_ffn"][0], w["ffn_w_gate"][0], w["ffn_w_up"][0], w["ffn_w_down"][0], name=f"{tag}_ffn0")

    cols1 = matmul([(x, x.shape[1], 0)], [w["w_in1"]], gain=p["norm_mix"][1], tm=tmm, name=f"{tag}_in1")
    if past is None:
        c_out = sb_flash(cols1, batch, seq, name=f"{tag}_sb_attn")
        g0 = jnp.zeros((batch, GLA_HEADS, GLA_DK, GLA_DV), f32)
        d_out, s_gla = gla_scan(cols1, g0, w["gla"], batch, seq, name=f"{tag}_gla_scan")
    else:
        c_out = sb_decode(cols1[:, C1_SQ:C1_SQ + SB_W], past["sb_k"], past["sb_v"], past["page_table"],
                          name=f"{tag}_sb_attn")
        la = gla_gate(cols1, w["gla"], name=f"{tag}_gla_gate")
        kw, vw = GLA_HEADS * GLA_DK, GLA_HEADS * GLA_DV
        d_out, s_gla = gla_step(cols1[:, C1_GQ:C1_GQ + kw], cols1[:, C1_GK:C1_GK + kw], la,
                                cols1[:, C1_GV:C1_GV + vw], cols1[:, C1_GOG:C1_GOG + vw], past["gla_state"],
                                w["gla"], name=f"{tag}_gla_step")
    state.update(sb_k=cols1[:, C1_SK:C1_SK + SB_W].reshape(batch, seq, SB_HEADS, SB_HEAD),
                 sb_v=cols1[:, C1_SV:C1_SV + SB_W].reshape(batch, seq, SB_HEADS, SB_HEAD), gla_state=s_gla)
    half = w["w_out1"].shape[0] // 2
    x = matmul([(c_out, half, 0), (d_out, half, 0)], [w["w_out1"][:half], w["w_out1"][half:]], res=x, tm=tmm,
               name=f"{tag}_out1")
    y = ffn(x, p["norm_ffn"][1], w["ffn_w_gate"][1], w["ffn_w_up"][1], w["ffn_w_down"][1],
            final_gain=p["norm_final"], name=f"{tag}_ffn1")
    return y.reshape(batch, seq, -1), state


def kernel(x_prompt, x_sample, cache_mla_ckv, cache_mla_kpe, cache_sb_k, cache_sb_v, state_rwkv, state_rwkv_shift, state_gla, page_table, w_in0, mla_q_norm, mla_w_qb, mla_kv_norm, mla_w_uk, mla_w_uv, rwkv_mu, rwkv_w0, rwkv_w2, rwkv_a0, rwkv_a2, rwkv_g2, rwkv_k_k, rwkv_k_a, rwkv_r_k, rwkv_ln_w, rwkv_ln_b, w_out0, w_in1, gla_w_gup, gla_b_g, gla_norm, w_out1, norm_mix, norm_ffn, ffn_w_gate, ffn_w_up, ffn_w_down, norm_final):
    p = dict(w_in0=w_in0, mla_q_norm=mla_q_norm, mla_w_qb=mla_w_qb, mla_kv_norm=mla_kv_norm, mla_w_uk=mla_w_uk,
             mla_w_uv=mla_w_uv, rwkv_mu=rwkv_mu, rwkv_w0=rwkv_w0, rwkv_w2=rwkv_w2, rwkv_a0=rwkv_a0,
             rwkv_a2=rwkv_a2, rwkv_g2=rwkv_g2, rwkv_k_k=rwkv_k_k, rwkv_k_a=rwkv_k_a, rwkv_r_k=rwkv_r_k,
             rwkv_ln_w=rwkv_ln_w, rwkv_ln_b=rwkv_ln_b, w_out0=w_out0, w_in1=w_in1, gla_w_gup=gla_w_gup,
             gla_b_g=gla_b_g, gla_norm=gla_norm, w_out1=w_out1, norm_mix=norm_mix, norm_ffn=norm_ffn,
             ffn_w_gate=ffn_w_gate, ffn_w_up=ffn_w_up, ffn_w_down=ffn_w_down, norm_final=norm_final)
    w = _prepare(p)
    b, t, d = x_prompt.shape
    db, dt, _ = x_sample.shape
    assert dt == 1
    past_len = page_table.shape[1] * PAGE_SIZE
    y_p, sp = _trunk(x_prompt.reshape(b * t, d), jnp.arange(t, dtype=jnp.int32), None, p, w, b, t, "p")
    past = dict(mla_ckv=cache_mla_ckv, mla_kpe=cache_mla_kpe, sb_k=cache_sb_k, sb_v=cache_sb_v,
                rwkv_state=state_rwkv, rwkv_shift=state_rwkv_shift, gla_state=state_gla, page_table=page_table)
    y_s, ss = _trunk(x_sample.reshape(db, d), past_len + jnp.arange(1, dtype=jnp.int32), past, p, w, db, 1, "s")
    keys = ("mla_ckv", "mla_kpe", "rwkv_state", "rwkv_shift", "sb_k", "sb_v", "gla_state")
    return (y_p, y_s) + tuple(sp[k] for k in keys) + tuple(ss[k] for k in keys)
```

```python
import functools
import math

import jax
import jax.numpy as jnp
from jax import lax
from jax.experimental import pallas as pl
from jax.experimental.pallas import tpu as pltpu

f32 = jnp.float32
bf16 = jnp.bfloat16

PAGE_SIZE = 128
MLA_HEADS = 8
MLA_NOPE = 128
MLA_ROPE = 64
MLA_V = 128
MLA_Q_RANK = 512
MLA_KV_RANK = 512
MLA_SCALE = (MLA_NOPE + MLA_ROPE) ** -0.5
ROPE_THETA = 10000.0
RWKV_HEADS = 16
RWKV_HEAD = 64
RWKV_W = RWKV_HEADS * RWKV_HEAD
RWKV_DECAY_RANK = 64
RWKV_A_RANK = 64
RWKV_GATE_RANK = 160
RWKV_GN_EPS = 64e-5
RWKV_COLS = 3 * RWKV_W + RWKV_DECAY_RANK + RWKV_A_RANK + RWKV_GATE_RANK
SB_HEADS = 16
SB_HEAD = 64
SB_W = SB_HEADS * SB_HEAD
SB_SCALE = SB_HEAD ** -0.5
GLA_HEADS = 4
GLA_DK = 128
GLA_DV = 256
GLA_GATE_RANK = 16
GLA_GATE_NORM = 16.0
NORM_EPS = 1e-6

LANES = 128
VMEM_LIMIT_BYTES = 56 * 1024 * 1024

C0_R, C0_K, C0_V = 0, 1024, 2048
C0_QA, C0_KVA = 3072, 3584
C0_KPE, C0_KPEROT = 4096, 4224
C0_WDAD = 4352
C0_GD = 4608
C0_N = 5120
C1_SQ, C1_SK, C1_SV = 0, 1024, 2048
C1_GQ, C1_GK, C1_GV = 3072, 3584, 4096
C1_GOG = 5120
C1_GGD = 6144
C1_N = 6656

RWKV_CHUNK = 64
GLA_CHUNK = 64
MLA_SLOT = 2 * LANES
ABS_SLOT = MLA_KV_RANK + LANES


def _cparams(*sem):
    return pltpu.CompilerParams(dimension_semantics=sem, vmem_limit_bytes=VMEM_LIMIT_BYTES)


def _dot(a, b):
    return jnp.dot(a, b, preferred_element_type=f32)


def _dot_nt(a, b):
    return lax.dot_general(a, b, (((1,), (1,)), ((), ())), preferred_element_type=f32)


def _dot_tn(a, b):
    return lax.dot_general(a, b, (((0,), (0,)), ((), ())), preferred_element_type=f32)


def _split_dot_right(x, m):
    hi = x.astype(bf16)
    lo = (x - hi.astype(f32)).astype(bf16)
    return _dot(hi, m) + _dot(lo, m)


def _split_dot_left(m, x):
    hi = x.astype(bf16)
    lo = (x - hi.astype(f32)).astype(bf16)
    return _dot(m, hi) + _dot(m, lo)


def _sigmoid(x):
    return 1.0 / (1.0 + jnp.exp(-x))


def _softplus(x):
    return jnp.maximum(x, 0.0) + jnp.log(1.0 + jnp.exp(-jnp.abs(x)))


def _rms(x, gain):
    return x * lax.rsqrt(jnp.mean(x * x, axis=-1, keepdims=True) + NORM_EPS) * gain


def _mm_body(*refs, n_lhs, norm, res):
    lhs = refs[:n_lhs]
    ws = refs[n_lhs:2 * n_lhs]
    pos = 2 * n_lhs
    g_ref = refs[pos] if norm else None
    pos += int(norm)
    r_ref = refs[pos] if res else None
    pos += int(res)
    o_ref = refs[pos]
    hs = refs[pos + 1:pos + 1 + n_lhs]

    @pl.when(pl.program_id(1) == 0)
    def _():
        for i in range(n_lhs):
            x = lhs[i][...].astype(f32)
            if norm and i == 0:
                x = _rms(x, g_ref[...])
            hs[i][...] = x.astype(bf16)

    acc = _dot(hs[0][...], ws[0][...])
    for i in range(1, n_lhs):
        acc = acc + _dot(hs[i][...], ws[i][...])
    if res:
        acc = acc + r_ref[...]
    o_ref[...] = acc.astype(o_ref.dtype)


def matmul(lhs_list, w_list, *, gain=None, res=None, out_dtype=f32, tm=512, tn=512, name):
    m = lhs_list[0][0].shape[0]
    n = w_list[0].shape[1]
    tm, tn = min(tm, m), min(tn, n)
    assert m % tm == 0 and n % tn == 0
    in_specs, args, scratch = [], [], []
    for arr, k, cb in lhs_list:
        in_specs.append(pl.BlockSpec((tm, k), lambda i, j, cb=cb: (i, cb)))
        args.append(arr)
        scratch.append(pltpu.VMEM((tm, k), bf16))
    for (arr, k, cb), w in zip(lhs_list, w_list):
        assert w.shape[0] == k
        in_specs.append(pl.BlockSpec((k, tn), lambda i, j: (0, j)))
        args.append(w)
    if gain is not None:
        in_specs.append(pl.BlockSpec((1, lhs_list[0][1]), lambda i, j: (0, 0)))
        args.append(gain.reshape(1, -1))
    if res is not None:
        in_specs.append(pl.BlockSpec((tm, tn), lambda i, j: (i, j)))
        args.append(res)
    body = functools.partial(_mm_body, n_lhs=len(lhs_list), norm=gain is not None, res=res is not None)
    return pl.pallas_call(
        body,
        out_shape=jax.ShapeDtypeStruct((m, n), out_dtype),
        grid=(m // tm, n // tn),
        in_specs=in_specs,
        out_specs=pl.BlockSpec((tm, tn), lambda i, j: (i, j)),
        scratch_shapes=scratch,
        compiler_params=_cparams("parallel", "arbitrary"),
        name=name,
    )(*args)


def _hmm_body(x_ref, w_ref, o_ref):
    o_ref[...] = _dot(x_ref[...].astype(bf16), w_ref[0]).astype(o_ref.dtype)


def head_matmul(x, w, *, name):
    m = x.shape[0]
    h, k, n = w.shape
    return pl.pallas_call(
        _hmm_body,
        out_shape=jax.ShapeDtypeStruct((m, h * n), f32),
        grid=(h,),
        in_specs=[pl.BlockSpec((m, k), lambda i: (0, i)), pl.BlockSpec((1, k, n), lambda i: (i, 0, 0))],
        out_specs=pl.BlockSpec((m, n), lambda i: (0, i)),
        compiler_params=_cparams("arbitrary"),
        name=name,
    )(x, w)


def _ffn_body(*refs, final):
    x_ref, g_ref, wg_ref, wu_ref, wd_ref = refs[:5]
    gf_ref = refs[5] if final else None
    o_ref, h_ref, acc_ref = refs[5 + int(final):]
    f = pl.program_id(1)

    @pl.when(f == 0)
    def _():
        h_ref[...] = _rms(x_ref[...], g_ref[...]).astype(bf16)
        acc_ref[...] = jnp.zeros_like(acc_ref)

    h = h_ref[...]
    a = _dot(h, wg_ref[...])
    u = _dot(h, wu_ref[...])
    s = (a * _sigmoid(a) * u).astype(bf16)
    acc_ref[...] += _dot(s, wd_ref[...])

    @pl.when(f == pl.num_programs(1) - 1)
    def _():
        y = x_ref[...] + acc_ref[...]
        if final:
            y = _rms(y, gf_ref[...])
        o_ref[...] = y


def ffn(x, gain, wg, wu, wd, *, final_gain=None, tm=512, tf=512, name):
    m, d = x.shape
    dff = wg.shape[1]
    tm = min(tm, m)
    assert m % tm == 0 and dff % tf == 0
    in_specs = [
        pl.BlockSpec((tm, d), lambda i, j: (i, 0)),
        pl.BlockSpec((1, d), lambda i, j: (0, 0)),
        pl.BlockSpec((d, tf), lambda i, j: (0, j)),
        pl.BlockSpec((d, tf), lambda i, j: (0, j)),
        pl.BlockSpec((tf, d), lambda i, j: (j, 0)),
    ]
    args = [x, gain.reshape(1, d), wg, wu, wd]
    if final_gain is not None:
        in_specs.append(pl.BlockSpec((1, d), lambda i, j: (0, 0)))
        args.append(final_gain.reshape(1, d))
    return pl.pallas_call(
        functools.partial(_ffn_body, final=final_gain is not None),
        out_shape=jax.ShapeDtypeStruct((m, d), f32),
        grid=(m // tm, dff // tf),
        in_specs=in_specs,
        out_specs=pl.BlockSpec((tm, d), lambda i, j: (i, 0)),
        scratch_shapes=[pltpu.VMEM((tm, d), bf16), pltpu.VMEM((tm, d), f32)],
        compiler_params=_cparams("parallel", "arbitrary"),
        name=name,
    )(*args)


def _mla_prep_body(*refs, absorbed):
    (qa_ref, kva_ref, kpe_ref, kperot_ref, cos_ref, sin_ref, qg_ref, wq_ref, kg_ref, wkv_ref) = refs[:10]
    outs = refs[10:]
    cos, sin = cos_ref[...], sin_ref[...]
    qn = _rms(qa_ref[...], qg_ref[...]).astype(bf16)
    qall = _dot(qn, wq_ref[...])
    hw = MLA_HEADS * LANES
    nope, pe, rot = qall[:, :hw], qall[:, hw:2 * hw], qall[:, 2 * hw:]
    cos8 = jnp.concatenate([cos] * MLA_HEADS, axis=1)
    sin8 = jnp.concatenate([sin] * MLA_HEADS, axis=1)
    roped = pe * cos8 + rot * sin8
    ckv = _rms(kva_ref[...], kg_ref[...])
    kpe = kpe_ref[...] * cos + kperot_ref[...] * sin
    if absorbed:
        qcat_ref, ckv_ref, kpe_out_ref, knew_ref = outs
        pieces = []
        for h in range(MLA_HEADS):
            qabs = _dot(nope[:, h * LANES:(h + 1) * LANES].astype(bf16), wkv_ref[h])
            pieces += [qabs, roped[:, h * LANES:(h + 1) * LANES]]
        qcat_ref[...] = jnp.concatenate(pieces, axis=1).astype(bf16)
        knew_ref[...] = jnp.concatenate([ckv, kpe], axis=1)
    else:
        qcat_ref, kcat_ref, v_ref, ckv_ref, kpe_out_ref = outs
        kv = _dot(ckv.astype(bf16), wkv_ref[...])
        qp, kp = [], []
        for h in range(MLA_HEADS):
            sl = slice(h * LANES, (h + 1) * LANES)
            qp += [nope[:, sl], roped[:, sl]]
            kp += [kv[:, sl], kpe]
        qcat_ref[...] = jnp.concatenate(qp, axis=1).astype(bf16)
        kcat_ref[...] = jnp.concatenate(kp, axis=1).astype(bf16)
        v_ref[...] = kv[:, hw:].astype(bf16)
    ckv_ref[...] = ckv
    kpe_out_ref[...] = kpe[:, :MLA_ROPE]


def mla_prep(cols0, cos, sin, q_gain, wq_all, kv_gain, wkv, *, absorbed, tm=256, name):
    m = cols0.shape[0]
    tm = min(tm, m)
    assert m % tm == 0
    row = lambda w, cb: pl.BlockSpec((tm, w), lambda i, cb=cb: (i, cb))
    full = lambda a: pl.BlockSpec(a.shape, lambda i, nd=a.ndim: (0,) * nd)
    qg, kg = q_gain.reshape(1, -1), kv_gain.reshape(1, -1)
    in_specs = [
        row(MLA_Q_RANK, C0_QA // MLA_Q_RANK), row(MLA_KV_RANK, C0_KVA // MLA_KV_RANK),
        row(LANES, C0_KPE // LANES), row(LANES, C0_KPEROT // LANES),
        row(LANES, 0), row(LANES, 0), full(qg), full(wq_all), full(kg), full(wkv),
    ]
    if absorbed:
        out_shape = [jax.ShapeDtypeStruct((m, MLA_HEADS * ABS_SLOT), bf16),
                     jax.ShapeDtypeStruct((m, MLA_KV_RANK), f32),
                     jax.ShapeDtypeStruct((m, MLA_ROPE), f32),
                     jax.ShapeDtypeStruct((m, ABS_SLOT), f32)]
        out_specs = [row(MLA_HEADS * ABS_SLOT, 0), row(MLA_KV_RANK, 0), row(MLA_ROPE, 0), row(ABS_SLOT, 0)]
    else:
        out_shape = [jax.ShapeDtypeStruct((m, MLA_HEADS * MLA_SLOT), bf16),
                     jax.ShapeDtypeStruct((m, MLA_HEADS * MLA_SLOT), bf16),
                     jax.ShapeDtypeStruct((m, MLA_HEADS * MLA_V), bf16),
                     jax.ShapeDtypeStruct((m, MLA_KV_RANK), f32),
                     jax.ShapeDtypeStruct((m, MLA_ROPE), f32)]
        out_specs = [row(MLA_HEADS * MLA_SLOT, 0), row(MLA_HEADS * MLA_SLOT, 0), row(MLA_HEADS * MLA_V, 0),
                     row(MLA_KV_RANK, 0), row(MLA_ROPE, 0)]
    return pl.pallas_call(
        functools.partial(_mla_prep_body, absorbed=absorbed),
        out_shape=out_shape,
        grid=(m // tm,),
        in_specs=in_specs,
        out_specs=out_specs,
        compiler_params=_cparams("parallel"),
        name=name,
    )(cols0, cols0, cols0, cols0, cos, sin, qg, wq_all, kg, wkv)


def _causal_pairs(nblk, reverse):
    qi, kj = [], []
    for i in range(nblk):
        ks = range(i, -1, -1) if reverse else range(i + 1)
        for j in ks:
            qi.append(i)
            kj.append(j)
    return jnp.asarray(qi, jnp.int32), jnp.asarray(kj, jnp.int32)


def _mla_flash_body(qi_ref, kj_ref, q_ref, k_ref, v_ref, o_ref, m_ref, l_ref, acc_ref, *, hp):
    p = pl.program_id(2)
    i, j = qi_ref[p], kj_ref[p]
    heads = range(hp)

    @pl.when(j == 0)
    def _():
        m_ref[...] = jnp.full_like(m_ref, -jnp.inf)
        l_ref[...] = jnp.zeros_like(l_ref)
        acc_ref[...] = jnp.zeros_like(acc_ref)

    def step(diagonal):
        q, k, v = q_ref[...], k_ref[...], v_ref[...]
        qk = [slice(h * MLA_SLOT, (h + 1) * MLA_SLOT) for h in heads]
        s = [_dot_nt(q[:, qk[h]], k[:, qk[h]]) * MLA_SCALE for h in heads]
        if diagonal:
            keep = (lax.broadcasted_iota(jnp.int32, s[0].shape, 1) <= lax.broadcasted_iota(jnp.int32, s[0].shape, 0))
            s = [jnp.where(keep, x, -jnp.inf) for x in s]
        m_prev = [m_ref[h] for h in heads]
        m_new = [jnp.maximum(m_prev[h], jnp.max(s[h], axis=-1, keepdims=True)) for h in heads]
        alpha = [jnp.exp(m_prev[h] - m_new[h]) for h in heads]
        pr = [jnp.exp(s[h] - m_new[h]) for h in heads]
        for h in heads:
            l_ref[h] = l_ref[h] * alpha[h] + jnp.sum(pr[h], axis=-1, keepdims=True)
            acc_ref[h] = acc_ref[h] * alpha[h] + _dot(pr[h].astype(bf16), v[:, h * MLA_V:(h + 1) * MLA_V])
            m_ref[h] = m_new[h]

    @pl.when(j < i)
    def _():
        step(False)

    @pl.when(j == i)
    def _():
        step(True)
        o_ref[...] = jnp.concatenate([acc_ref[h] / l_ref[h] for h in heads], axis=1)


def mla_flash(qcat, kcat, v, batch, seq, *, tq=512, hp=4, name):
    tq = min(tq, seq)
    nblk = seq // tq
    qi, kj = _causal_pairs(nblk, reverse=False)
    grid_spec = pltpu.PrefetchScalarGridSpec(
        num_scalar_prefetch=2,
        grid=(batch, MLA_HEADS // hp, int(qi.shape[0])),
        in_specs=[
            pl.BlockSpec((tq, hp * MLA_SLOT), lambda b, h, p, qi, kj: (b * nblk + qi[p], h)),
            pl.BlockSpec((tq, hp * MLA_SLOT), lambda b, h, p, qi, kj: (b * nblk + kj[p], h)),
            pl.BlockSpec((tq, hp * MLA_V), lambda b, h, p, qi, kj: (b * nblk + kj[p], h)),
        ],
        out_specs=pl.BlockSpec((tq, hp * MLA_V), lambda b, h, p, qi, kj: (b * nblk + qi[p], h)),
        scratch_shapes=[pltpu.VMEM((hp, tq, 1), f32), pltpu.VMEM((hp, tq, 1), f32),
                        pltpu.VMEM((hp, tq, MLA_V), f32)],
    )
    return pl.pallas_call(
        functools.partial(_mla_flash_body, hp=hp),
        out_shape=jax.ShapeDtypeStruct((batch * seq, MLA_HEADS * MLA_V), f32),
        grid_spec=grid_spec,
        compiler_params=_cparams("parallel", "parallel", "arbitrary"),
        name=name,
    )(qi, kj, qcat, kcat, v)


def _sb_flash_body(qi_ref, kj_ref, q_ref, k_ref, v_ref, u_ref, o_ref, acc_ref, run_ref, *, tq, hp):
    p = pl.program_id(2)
    i, j = qi_ref[p], kj_ref[p]

    heads = range(hp)

    def step(diagonal):
        q = (q_ref[...] * SB_SCALE).astype(bf16)
        k, v = k_ref[...].astype(bf16), v_ref[...].astype(bf16)
        upper = u_ref[...]
        sls = [slice(h * SB_HEAD, (h + 1) * SB_HEAD) for h in heads]
        z = [_dot_nt(q[:, sl], k[:, sl]) for sl in sls]
        log_fail = [-_softplus(x) for x in z]
        if diagonal:
            valid = lax.broadcasted_iota(jnp.int32, z[0].shape, 1) < lax.broadcasted_iota(jnp.int32, z[0].shape, 0)
            log_fail = [jnp.where(valid, x, 0.0) for x in log_fail]
        after = [_split_dot_right(x, upper) for x in log_fail]
        if diagonal:
            w = [jnp.where(valid, jnp.exp(z[h] + log_fail[h] + after[h]), 0.0) for h in heads]
        else:
            w = [jnp.exp(z[h] + log_fail[h] + after[h] + run_ref[h]) for h in heads]
        for h in heads:
            part = _dot(w[h].astype(bf16), v[:, sls[h]])
            total = jnp.sum(log_fail[h], axis=-1, keepdims=True)
            if diagonal:
                acc_ref[h] = part
                run_ref[h] = total
            else:
                acc_ref[h] += part
                run_ref[h] += total

    @pl.when(j == i)
    def _():
        step(True)

    @pl.when(j < i)
    def _():
        step(False)

    @pl.when(j == 0)
    def _():
        o_ref[...] = jnp.concatenate([acc_ref[h] for h in heads], axis=1)


def sb_flash(cols1, batch, seq, *, tq=512, hp=2, name):
    tq = min(tq, seq)
    nblk = seq // tq
    qi, kj = _causal_pairs(nblk, reverse=True)
    w = hp * SB_HEAD
    nq, nk, nv = C1_SQ // w, C1_SK // w, C1_SV // w
    upper = (lax.broadcasted_iota(jnp.int32, (tq, tq), 0) > lax.broadcasted_iota(jnp.int32, (tq, tq), 1)).astype(bf16)
    grid_spec = pltpu.PrefetchScalarGridSpec(
        num_scalar_prefetch=2,
        grid=(batch, SB_HEADS // hp, int(qi.shape[0])),
        in_specs=[
            pl.BlockSpec((tq, w), lambda b, h, p, qi, kj: (b * nblk + qi[p], nq + h)),
            pl.BlockSpec((tq, w), lambda b, h, p, qi, kj: (b * nblk + kj[p], nk + h)),
            pl.BlockSpec((tq, w), lambda b, h, p, qi, kj: (b * nblk + kj[p], nv + h)),
            pl.BlockSpec((tq, tq), lambda b, h, p, qi, kj: (0, 0)),
        ],
        out_specs=pl.BlockSpec((tq, w), lambda b, h, p, qi, kj: (b * nblk + qi[p], h)),
        scratch_shapes=[pltpu.VMEM((hp, tq, SB_HEAD), f32), pltpu.VMEM((hp, tq, 1), f32)],
    )
    return pl.pallas_call(
        functools.partial(_sb_flash_body, tq=tq, hp=hp),
        out_shape=jax.ShapeDtypeStruct((batch * seq, SB_W), f32),
        grid_spec=grid_spec,
        compiler_params=_cparams("parallel", "parallel", "arbitrary"),
        name=name,
    )(qi, kj, cols1, cols1, cols1, upper)


def _rwkv_gates_body(*refs, seq_rows):
    (r_ref, k_ref, v_ref, wdad_ref, gd_ref) = refs[:5]
    pos = 5
    if seq_rows == 1:
        prevs = [ref[...] for ref in refs[pos:pos + 5]]
        pos += 5
    else:
        tails = refs[pos:pos + 5]
        firsts = refs[pos + 5:pos + 10]
        pos += 10
    (mu_r, mu_k, mu_v, mu_wdad, mu_gd, w0_ref, w2_ref, a0_ref, a2_ref, g2_ref, kk_ref, ka_ref, ones_ref) = refs[pos:pos + 13]
    (r_out, k_out, v_out, kkn_out, a_out, lw_out, g_out) = refs[pos + 13:]
    cur = [r_ref[...], k_ref[...], v_ref[...], wdad_ref[...], gd_ref[...]]
    if seq_rows != 1:
        tm = cur[0].shape[0]
        starts_seq = (pl.program_id(0) * tm) % seq_rows == 0
        prevs = []
        for x, tail, first in zip(cur, tails, firsts):
            carry = jnp.where(starts_seq, first[0], tail[7:8, :])
            rolled = pltpu.roll(x, 1, 0)
            rowid = lax.broadcasted_iota(jnp.int32, x.shape, 0)
            prevs.append(jnp.where(rowid == 0, carry, rolled))
    mus = [mu_r[...], mu_k[...], mu_v[...], mu_wdad[...], mu_gd[...]]
    xr, xk, xv, xwdad, xgd = [c + (p - c) * m for c, p, m in zip(cur, prevs, mus)]
    w_log = -_softplus(-(w0_ref[...] + _dot(jnp.tanh(xwdad).astype(bf16), w2_ref[...]))) - 0.5
    lw_out[...] = -jnp.exp(w_log)
    a = _sigmoid(a0_ref[...] + _dot(xwdad.astype(bf16), a2_ref[...]))
    g_out[...] = _dot(_sigmoid(xgd).astype(bf16), g2_ref[...])
    kk = xk * kk_ref[...]
    ssq = _split_dot_right(kk * kk, ones_ref[...])
    kkn_out[...] = kk / jnp.maximum(jnp.sqrt(ssq), 1e-12)
    k_out[...] = xk * (1.0 + (a - 1.0) * ka_ref[...])
    r_out[...] = xr
    v_out[...] = xv
    a_out[...] = a


def rwkv_gates(cols0, shift_p, seq_rows, prm, *, tm=256, name):
    m = cols0.shape[0]
    tm = min(tm, m, seq_rows) if seq_rows != 1 else min(tm, m)
    assert m % tm == 0 and (seq_rows == 1 or (seq_rows % tm == 0 and tm % 8 == 0))
    groups = [(RWKV_W, C0_R // RWKV_W), (RWKV_W, C0_K // RWKV_W), (RWKV_W, C0_V // RWKV_W),
              (LANES, C0_WDAD // LANES), (2 * LANES, C0_GD // (2 * LANES))]
    in_specs = [pl.BlockSpec((tm, w), lambda i, cb=cb: (i, cb)) for w, cb in groups]
    args = [cols0] * 5
    if seq_rows == 1:
        in_specs += [pl.BlockSpec((tm, w), lambda i, cb=cb: (i, cb)) for w, cb in groups]
        args += [shift_p] * 5
    else:
        per = tm // 8
        in_specs += [pl.BlockSpec((8, w), lambda i, cb=cb: (jnp.maximum(i * per - 1, 0), cb)) for w, cb in groups]
        args += [cols0] * 5
        shift3 = shift_p.reshape(shift_p.shape[0], 1, C0_N)
        in_specs += [pl.BlockSpec((1, 1, w), lambda i, cb=cb: ((i * tm) // seq_rows, 0, cb)) for w, cb in groups]
        args += [shift3] * 5
    small = [prm["mu_r"], prm["mu_k"], prm["mu_v"], prm["mu_wdad"], prm["mu_gd"], prm["w0"], prm["w2p"],
             prm["a0"], prm["a2p"], prm["g2p"], prm["k_k"], prm["k_a"], prm["head_ones"]]
    in_specs += [pl.BlockSpec(a.shape, lambda i: (0, 0)) for a in small]
    args += small
    out_spec = pl.BlockSpec((tm, RWKV_W), lambda i: (i, 0))
    return pl.pallas_call(
        functools.partial(_rwkv_gates_body, seq_rows=seq_rows),
        out_shape=[jax.ShapeDtypeStruct((m, RWKV_W), f32)] * 7,
        grid=(m // tm,),
        in_specs=in_specs,
        out_specs=[out_spec] * 7,
        compiler_params=_cparams("parallel"),
        name=name,
    )(*args)


def _rwkv_scan_body(r_ref, k_ref, v_ref, kk_ref, a_ref, lw_ref, g_ref, s0_ref, rk_ref, lnw_ref, lnb_ref,
                    y_ref, s_out_ref, s_ref):
    c = pl.program_id(1)
    C = r_ref.shape[0]

    @pl.when(c == 0)
    def _():
        s_ref[...] = s0_ref[0]

    rowi = lax.broadcasted_iota(jnp.int32, (C, C), 0)
    coli = lax.broadcasted_iota(jnp.int32, (C, C), 1)
    incl = coli <= rowi
    strict = coli < rowi
    lower = incl.astype(bf16)
    r, k, v, kk, a, lw, g = (x[...] for x in (r_ref, k_ref, v_ref, kk_ref, a_ref, lw_ref, g_ref))
    cum = _split_dot_left(lower, lw)
    gam = jnp.exp(cum)
    ginv = jnp.exp(-cum)
    a_m = -kk * jnp.exp(cum - lw)
    b_m = kk * a * ginv
    k_m = k * ginv
    r_m = r * gam
    cum_last = cum[C - 1:C, :]
    g_last = jnp.exp(cum_last)
    b_end = kk * a * jnp.exp(cum_last - cum)
    k_end = k * jnp.exp(cum_last - cum)
    bonus_w = r * k * rk_ref[...]
    lnw, lnb = lnw_ref[...], lnb_ref[...]
    heads = range(RWKV_HEADS)
    sls = [slice(h * RWKV_HEAD, (h + 1) * RWKV_HEAD) for h in heads]
    ar = [jnp.concatenate([a_m[:, sl], r_m[:, sl]], axis=0).astype(bf16) for sl in sls]
    vh = [v[:, sl].astype(bf16) for sl in sls]
    s_old = [s_ref[h] for h in heads]
    g_b = [_dot_nt(ar[h], b_m[:, sls[h]].astype(bf16)) for h in heads]
    g_k = [_dot_nt(ar[h], k_m[:, sls[h]].astype(bf16)) for h in heads]
    g_s = [_dot_nt(ar[h], s_old[h].astype(bf16)) for h in heads]
    x = [g_s[h][:C] + _dot(jnp.where(strict, g_k[h][:C], 0.0).astype(bf16), vh[h]) for h in heads]
    n = [jnp.where(strict, g_b[h][:C], 0.0).astype(bf16) for h in heads]
    steps = int(math.log2(C))
    for it in range(steps):
        x = [x[h] + _dot(n[h], x[h].astype(bf16)) for h in heads]
        if it + 1 < steps:
            n = [_dot(n[h], n[h]).astype(bf16) for h in heads]
    xb = [x[h].astype(bf16) for h in heads]
    y = [g_s[h][C:] + _dot(jnp.where(incl, g_b[h][C:], 0.0).astype(bf16), xb[h])
         + _dot(jnp.where(incl, g_k[h][C:], 0.0).astype(bf16), vh[h]) for h in heads]
    for h in heads:
        pv = jnp.concatenate([xb[h], vh[h]], axis=0)
        bk_end = jnp.concatenate([b_end[:, sls[h]], k_end[:, sls[h]]], axis=0).astype(bf16)
        s_ref[h] = s_old[h] * g_last[:, sls[h]] + _dot_tn(pv, bk_end)
    pieces = []
    for h in heads:
        sl = sls[h]
        mean = jnp.mean(y[h], axis=-1, keepdims=True)
        var = jnp.mean(jnp.square(y[h] - mean), axis=-1, keepdims=True)
        yn = (y[h] - mean) * lax.rsqrt(var + RWKV_GN_EPS) * lnw[:, sl] + lnb[:, sl]
        bonus = jnp.sum(bonus_w[:, sl], axis=-1, keepdims=True) * v[:, sl]
        pieces.append((yn + bonus) * g[:, sl])
    y_ref[...] = jnp.concatenate(pieces, axis=1)

    @pl.when(c == pl.num_programs(1) - 1)
    def _():
        s_out_ref[0] = s_ref[...]


def rwkv_scan(gates, s0, prm, batch, seq, *, name):
    C = min(RWKV_CHUNK, seq)
    nch = seq // C
    tok = pl.BlockSpec((C, RWKV_W), lambda b, c: (b * nch + c, 0))
    st = pl.BlockSpec((1, RWKV_HEADS, RWKV_HEAD, RWKV_HEAD), lambda b, c: (b, 0, 0, 0))
    vec = pl.BlockSpec((1, RWKV_W), lambda b, c: (0, 0))
    r, k, v, kkn, a, lw, g = gates
    return pl.pallas_call(
        _rwkv_scan_body,
        out_shape=[jax.ShapeDtypeStruct((batch * seq, RWKV_W), f32),
                   jax.ShapeDtypeStruct((batch, RWKV_HEADS, RWKV_HEAD, RWKV_HEAD), f32)],
        grid=(batch, nch),
        in_specs=[tok] * 7 + [st, vec, vec, vec],
        out_specs=[tok, st],
        scratch_shapes=[pltpu.VMEM((RWKV_HEADS, RWKV_HEAD, RWKV_HEAD), f32)],
        compiler_params=_cparams("parallel", "arbitrary"),
        name=name,
    )(r, k, v, kkn, a, lw, g, s0, prm["r_k"], prm["ln_w"], prm["ln_b"])


def _rwkv_step_body(r_ref, k_ref, v_ref, kk_ref, a_ref, lw_ref, g_ref, s0_ref, rk_ref, lnw_ref, lnb_ref,
                    y_ref, s_out_ref):
    n = RWKV_HEAD
    eye = (lax.broadcasted_iota(jnp.int32, (n, n), 0) == lax.broadcasted_iota(jnp.int32, (n, n), 1)).astype(f32)
    r, k, v, kk, a, lw, g = (x[...] for x in (r_ref, k_ref, v_ref, kk_ref, a_ref, lw_ref, g_ref))
    s = s0_ref[...]
    sa = jnp.sum(s * (-kk), axis=-1, keepdims=True)
    vcol = jnp.sum(eye * v, axis=-1, keepdims=True)
    s = s * jnp.exp(lw) + sa * (kk * a) + vcol * k
    s_out_ref[...] = s
    ycol = jnp.sum(s * r, axis=-1, keepdims=True)
    y = jnp.sum(eye * ycol, axis=-2, keepdims=True)
    mean = jnp.mean(y, axis=-1, keepdims=True)
    var = jnp.mean(jnp.square(y - mean), axis=-1, keepdims=True)
    y = (y - mean) * lax.rsqrt(var + RWKV_GN_EPS) * lnw_ref[...] + lnb_ref[...]
    bonus = jnp.sum(r * k * rk_ref[...], axis=-1, keepdims=True) * v
    y_ref[...] = (y + bonus) * g


def rwkv_step(gates, s0, prm, *, bs=8, name):
    m = s0.shape[0]
    bs = min(bs, m)
    assert m % bs == 0
    h4 = lambda x: x.reshape(-1, RWKV_HEADS, 1, RWKV_HEAD)
    vec = pl.BlockSpec((bs, RWKV_HEADS, 1, RWKV_HEAD), lambda i: (i, 0, 0, 0))
    st = pl.BlockSpec((bs, RWKV_HEADS, RWKV_HEAD, RWKV_HEAD), lambda i: (i, 0, 0, 0))
    par = pl.BlockSpec((1, RWKV_HEADS, 1, RWKV_HEAD), lambda i: (0, 0, 0, 0))
    y, s = pl.pallas_call(
        _rwkv_step_body,
        out_shape=[jax.ShapeDtypeStruct((m, RWKV_HEADS, 1, RWKV_HEAD), f32), jax.ShapeDtypeStruct(s0.shape, f32)],
        grid=(m // bs,),
        in_specs=[vec] * 7 + [st, par, par, par],
        out_specs=[vec, st],
        compiler_params=_cparams("parallel"),
        name=name,
    )(*[h4(x) for x in gates], s0, h4(prm["r_k"]), h4(prm["ln_w"]), h4(prm["ln_b"]))
    return y.reshape(m, RWKV_W), s


def _gla_log_gate(ggd, wgup, bg):
    x = _dot(ggd.astype(bf16), wgup) + bg
    return -_softplus(-x) / GLA_GATE_NORM


def _gla_out(o, gain, gog):
    return _rms(o, gain) * (gog * _sigmoid(gog))


def _gla_scan_body(q_ref, k_ref, v_ref, ggd_ref, gog_ref, s0_ref, wgup_ref, bg_ref, gn_ref,
                   d_ref, s_out_ref, s_ref):
    c = pl.program_id(1)
    C = q_ref.shape[0]

    @pl.when(c == 0)
    def _():
        s_ref[...] = s0_ref[0]

    rowi = lax.broadcasted_iota(jnp.int32, (C, C), 0)
    coli = lax.broadcasted_iota(jnp.int32, (C, C), 1)
    incl = coli <= rowi
    lower = incl.astype(bf16)
    eye = (lax.broadcasted_iota(jnp.int32, (GLA_DK, GLA_DK), 0)
           == lax.broadcasted_iota(jnp.int32, (GLA_DK, GLA_DK), 1)).astype(f32)
    la = _gla_log_gate(ggd_ref[...], wgup_ref[...], bg_ref[...])
    b = _split_dot_left(lower, la)
    mid = (C // 2) // 8 * 8
    b_mid = b[mid:mid + 1, :]
    b_last = b[C - 1:C, :]
    qs = q_ref[...] * GLA_DK ** -0.5
    k, v, gog = k_ref[...], v_ref[...], gog_ref[...]
    q_inter = (qs * jnp.exp(b)).astype(bf16)
    q_mid = (qs * jnp.exp(b - b_mid)).astype(bf16)
    k_mid = (k * jnp.exp(b_mid - b)).astype(bf16)
    k_end = (k * jnp.exp(b_last - b)).astype(bf16)
    e_last = jnp.exp(b_last)
    pieces = []
    for h in range(GLA_HEADS):
        ks = slice(h * GLA_DK, (h + 1) * GLA_DK)
        vs = slice(h * GLA_DV, (h + 1) * GLA_DV)
        s = s_ref[h]
        vh = v[:, vs].astype(bf16)
        att = jnp.where(incl, _dot_nt(q_mid[:, ks], k_mid[:, ks]), 0.0)
        o = _dot(q_inter[:, ks], s.astype(bf16)) + _dot(att.astype(bf16), vh)
        e_col = jnp.sum(eye * e_last[:, ks], axis=-1, keepdims=True)
        s_ref[h] = s * e_col + _dot_tn(k_end[:, ks], vh)
        pieces.append(_gla_out(o, gn_ref[...], gog[:, vs]))
    d_ref[...] = jnp.concatenate(pieces, axis=1)

    @pl.when(c == pl.num_programs(1) - 1)
    def _():
        s_out_ref[0] = s_ref[...]


def gla_scan(cols1, s0, prm, batch, seq, *, name):
    C = min(GLA_CHUNK, seq)
    nch = seq // C
    kw, vw = GLA_HEADS * GLA_DK, GLA_HEADS * GLA_DV
    tok = lambda w, off: pl.BlockSpec((C, w), lambda b, c, cb=off // w: (b * nch + c, cb))
    st = pl.BlockSpec((1, GLA_HEADS, GLA_DK, GLA_DV), lambda b, c: (b, 0, 0, 0))
    full = lambda a: pl.BlockSpec(a.shape, lambda b, c: (0, 0))
    return pl.pallas_call(
        _gla_scan_body,
        out_shape=[jax.ShapeDtypeStruct((batch * seq, vw), f32),
                   jax.ShapeDtypeStruct((batch, GLA_HEADS, GLA_DK, GLA_DV), f32)],
        grid=(batch, nch),
        in_specs=[tok(kw, C1_GQ), tok(kw, C1_GK), tok(vw, C1_GV), tok(LANES, C1_GGD), tok(vw, C1_GOG), st,
                  full(prm["wgup"]), full(prm["bg"]), full(prm["gn"])],
        out_specs=[pl.BlockSpec((C, vw), lambda b, c: (b * nch + c, 0)), st],
        scratch_shapes=[pltpu.VMEM((GLA_HEADS, GLA_DK, GLA_DV), f32)],
        compiler_params=_cparams("parallel", "arbitrary"),
        name=name,
    )(cols1, cols1, cols1, cols1, cols1, s0, prm["wgup"], prm["bg"], prm["gn"])


def _gla_gate_body(ggd_ref, wgup_ref, bg_ref, la_ref):
    la_ref[...] = _gla_log_gate(ggd_ref[...], wgup_ref[...], bg_ref[...])


def gla_gate(cols1, prm, *, name):
    m = cols1.shape[0]
    kw = GLA_HEADS * GLA_DK
    return pl.pallas_call(
        _gla_gate_body,
        out_shape=jax.ShapeDtypeStruct((m, kw), f32),
        grid=(1,),
        in_specs=[pl.BlockSpec((m, LANES), lambda i: (0, C1_GGD // LANES)),
                  pl.BlockSpec(prm["wgup"].shape, lambda i: (0, 0)), pl.BlockSpec(prm["bg"].shape, lambda i: (0, 0))],
        out_specs=pl.BlockSpec((m, kw), lambda i: (0, 0)),
        compiler_params=_cparams("arbitrary"),
        name=name,
    )(cols1, prm["wgup"], prm["bg"])


def _gla_step_body(q_ref, k_ref, la_ref, v_ref, gog_ref, s0_ref, gn_ref, d_ref, s_out_ref):
    n = GLA_DK
    eye = (lax.broadcasted_iota(jnp.int32, (n, n), 0) == lax.broadcasted_iota(jnp.int32, (n, n), 1)).astype(f32)
    col = lambda x: jnp.sum(eye * x, axis=-1, keepdims=True)
    qc = col(q_ref[...] * GLA_DK ** -0.5)
    kc = col(k_ref[...])
    ec = col(jnp.exp(la_ref[...]))
    s = s0_ref[...] * ec + kc * v_ref[...]
    s_out_ref[...] = s
    o = jnp.sum(qc * s, axis=-2, keepdims=True)
    d_ref[...] = _gla_out(o, gn_ref[...], gog_ref[...])


def gla_step(q, k, la, v, gog, s0, prm, *, bs=8, name):
    m = s0.shape[0]
    bs = min(bs, m)
    assert m % bs == 0
    hk = lambda x: x.reshape(m, GLA_HEADS, 1, GLA_DK)
    hv = lambda x: x.reshape(m, GLA_HEADS, 1, GLA_DV)
    ks = pl.BlockSpec((bs, GLA_HEADS, 1, GLA_DK), lambda i: (i, 0, 0, 0))
    vs = pl.BlockSpec((bs, GLA_HEADS, 1, GLA_DV), lambda i: (i, 0, 0, 0))
    st = pl.BlockSpec((bs, GLA_HEADS, GLA_DK, GLA_DV), lambda i: (i, 0, 0, 0))
    gn = prm["gn"].reshape(1, 1, 1, GLA_DV)
    d, s = pl.pallas_call(
        _gla_step_body,
        out_shape=[jax.ShapeDtypeStruct((m, GLA_HEADS, 1, GLA_DV), f32), jax.ShapeDtypeStruct(s0.shape, f32)],
        grid=(m // bs,),
        in_specs=[ks, ks, ks, vs, vs, st, pl.BlockSpec(gn.shape, lambda i: (0, 0, 0, 0))],
        out_specs=[vs, st],
        compiler_params=_cparams("parallel"),
        name=name,
    )(hk(q), hk(k), hk(la), hv(v), hv(gog), s0, gn)
    return d.reshape(m, GLA_HEADS * GLA_DV), s


def _mla_decode_body(*refs, pages):
    pt_ref, q_ref, knew_ref = refs[:3]
    ckv_refs = refs[3:3 + pages]
    kpe_refs = refs[3 + pages:3 + 2 * pages]
    o_ref, m_ref, l_ref, acc_ref = refs[3 + 2 * pages:]
    j = pl.program_id(1)
    q = q_ref[0]
    q_abs, q_pe = q[:, :MLA_KV_RANK], q[:, MLA_KV_RANK:MLA_KV_RANK + MLA_ROPE]

    @pl.when(j == 0)
    def _():
        kn = knew_ref[0].astype(bf16).astype(f32)
        s_new = jnp.sum(q.astype(f32) * kn, axis=-1, keepdims=True) * MLA_SCALE
        m_ref[...] = s_new
        l_ref[...] = jnp.ones_like(l_ref)
        acc_ref[...] = jnp.broadcast_to(kn[:, :MLA_KV_RANK], acc_ref.shape)

    cks = [r[0].astype(bf16) for r in ckv_refs]
    scores = [(_dot_nt(q_abs, ck) + _dot(q_pe, kr[0].astype(bf16))) * MLA_SCALE
              for ck, kr in zip(cks, kpe_refs)]
    s = jnp.concatenate(scores, axis=1)
    m_prev = m_ref[...]
    m_new = jnp.maximum(m_prev, jnp.max(s, axis=-1, keepdims=True))
    alpha = jnp.exp(m_prev - m_new)
    pr = jnp.exp(s - m_new)
    l_ref[...] = l_ref[...] * alpha + jnp.sum(pr, axis=-1, keepdims=True)
    acc = acc_ref[...] * alpha
    for i, ck in enumerate(cks):
        acc = acc + _dot(pr[:, i * PAGE_SIZE:(i + 1) * PAGE_SIZE].astype(bf16), ck)
    acc_ref[...] = acc
    m_ref[...] = m_new

    @pl.when(j == pl.num_programs(1) - 1)
    def _():
        o_ref[0] = acc_ref[...] / l_ref[...]


def mla_decode(qcat, knew, cache_ckv, cache_kpe, page_table, *, pages=16, name):
    nseq, npages = page_table.shape
    pages = min(pages, npages)
    assert npages % pages == 0
    q3 = qcat.reshape(nseq, MLA_HEADS, ABS_SLOT)
    kn3 = knew.reshape(nseq, 1, ABS_SLOT)
    kpe_t = jnp.transpose(cache_kpe, (0, 2, 1))
    page = lambda r, c, i: pl.BlockSpec((1, r, c), lambda b, j, pt, i=i: (pt[b, j * pages + i], 0, 0))
    grid_spec = pltpu.PrefetchScalarGridSpec(
        num_scalar_prefetch=1,
        grid=(nseq, npages // pages),
        in_specs=[pl.BlockSpec((1, MLA_HEADS, ABS_SLOT), lambda b, j, pt: (b, 0, 0)),
                  pl.BlockSpec((1, 1, ABS_SLOT), lambda b, j, pt: (b, 0, 0))]
        + [page(PAGE_SIZE, MLA_KV_RANK, i) for i in range(pages)]
        + [page(MLA_ROPE, PAGE_SIZE, i) for i in range(pages)],
        out_specs=pl.BlockSpec((1, MLA_HEADS, MLA_KV_RANK), lambda b, j, pt: (b, 0, 0)),
        scratch_shapes=[pltpu.VMEM((MLA_HEADS, 1), f32), pltpu.VMEM((MLA_HEADS, 1), f32),
                        pltpu.VMEM((MLA_HEADS, MLA_KV_RANK), f32)],
    )
    lat = pl.pallas_call(
        functools.partial(_mla_decode_body, pages=pages),
        out_shape=jax.ShapeDtypeStruct((nseq, MLA_HEADS, MLA_KV_RANK), f32),
        grid_spec=grid_spec,
        compiler_params=_cparams("parallel", "arbitrary"),
        name=name,
    )(page_table, q3, kn3, *([cache_ckv] * pages), *([kpe_t] * pages))
    return lat.reshape(nseq, MLA_HEADS * MLA_KV_RANK)


def _sb_decode_body(*refs, pages):
    pt_ref, q_ref, upper_ref = refs[:3]
    k_refs = refs[3:3 + pages]
    v_refs = refs[3 + pages:3 + 2 * pages]
    o_ref, qb_ref, acc_ref, run_ref = refs[3 + 2 * pages:]
    j = pl.program_id(1)
    n = SB_HEAD
    eye = (lax.broadcasted_iota(jnp.int32, (n, n), 0) == lax.broadcasted_iota(jnp.int32, (n, n), 1)).astype(f32)

    @pl.when(j == 0)
    def _():
        q = q_ref[0]
        qcol = jnp.sum(eye[None] * q[:, None, :], axis=-1, keepdims=True)
        qb_ref[...] = jnp.broadcast_to(qcol, qb_ref.shape)
        acc_ref[...] = jnp.zeros_like(acc_ref)
        run_ref[...] = jnp.zeros_like(run_ref)

    qb = qb_ref[...]
    z = jnp.concatenate([jnp.sum(qb * k_ref[0], axis=1) for k_ref in k_refs], axis=0) * SB_SCALE
    log_fail = -_softplus(z)
    after = _split_dot_right(log_fail, upper_ref[...])
    total = jnp.sum(log_fail, axis=-1, keepdims=True)
    run = run_ref[...]
    ws = []
    for p in range(pages):
        rows = slice(p * SB_HEADS, (p + 1) * SB_HEADS)
        ws.append(jnp.exp(z[rows] + log_fail[rows] + after[rows] + run))
        run = run + total[rows]
    run_ref[...] = run
    for h in range(SB_HEADS):
        a = acc_ref[h]
        for p in range(pages):
            a = a + v_refs[p][0, h] * ws[p][h:h + 1, :]
        acc_ref[h] = a

    @pl.when(j == pl.num_programs(1) - 1)
    def _():
        ocol = jnp.sum(acc_ref[...], axis=-1, keepdims=True)
        o_ref[0] = jnp.sum(eye[None] * ocol, axis=1)


def sb_decode(q, cache_k, cache_v, page_table, *, pages=8, name):
    nseq, npages = page_table.shape
    pages = min(pages, npages)
    assert npages % pages == 0
    ck = jnp.transpose(cache_k, (0, 2, 3, 1))
    cv = jnp.transpose(cache_v, (0, 2, 3, 1))
    q3 = q.reshape(nseq, SB_HEADS, SB_HEAD)
    upper = (lax.broadcasted_iota(jnp.int32, (PAGE_SIZE, PAGE_SIZE), 0)
             > lax.broadcasted_iota(jnp.int32, (PAGE_SIZE, PAGE_SIZE), 1)).astype(bf16)
    page = lambda i: pl.BlockSpec(
        (1, SB_HEADS, SB_HEAD, PAGE_SIZE), lambda b, j, pt, i=i: (pt[b, npages - 1 - (j * pages + i)], 0, 0, 0))
    state = pltpu.VMEM((SB_HEADS, SB_HEAD, PAGE_SIZE), f32)
    grid_spec = pltpu.PrefetchScalarGridSpec(
        num_scalar_prefetch=1,
        grid=(nseq, npages // pages),
        in_specs=[pl.BlockSpec((1, SB_HEADS, SB_HEAD), lambda b, j, pt: (b, 0, 0)),
                  pl.BlockSpec((PAGE_SIZE, PAGE_SIZE), lambda b, j, pt: (0, 0))]
        + [page(i) for i in range(pages)] * 2,
        out_specs=pl.BlockSpec((1, SB_HEADS, SB_HEAD), lambda b, j, pt: (b, 0, 0)),
        scratch_shapes=[state, state, pltpu.VMEM((SB_HEADS, 1), f32)],
    )
    out = pl.pallas_call(
        functools.partial(_sb_decode_body, pages=pages),
        out_shape=jax.ShapeDtypeStruct((nseq, SB_HEADS, SB_HEAD), f32),
        grid_spec=grid_spec,
        compiler_params=_cparams("parallel", "arbitrary"),
        name=name,
    )(page_table, q3, upper, *([ck] * pages), *([cv] * pages))
    return out.reshape(nseq, SB_W)


def _place(width, pieces, dtype=None):
    lead = pieces[0][1].shape[:-1]
    out, pos = [], 0
    for off, arr in pieces:
        if off > pos:
            out.append(jnp.zeros(lead + (off - pos,), arr.dtype))
        out.append(arr)
        pos = off + arr.shape[-1]
    if width > pos:
        out.append(jnp.zeros(lead + (width - pos,), pieces[0][1].dtype))
    res = jnp.concatenate(out, axis=-1)
    return res if dtype is None else res.astype(dtype)


def _rot_half_cols(w):
    half = w.shape[-1] // 2
    return jnp.concatenate([-w[..., half:], w[..., :half]], axis=-1)


def _rwkv_col_pieces(x):
    w = RWKV_W
    return [(C0_R, x[..., :3 * w]), (C0_WDAD, x[..., 3 * w:3 * w + 128]), (C0_GD, x[..., 3 * w + 128:])]


def _prepare(p):
    out = {}
    mla_cols = 2 * MLA_Q_RANK + MLA_ROPE
    w_in0 = p["w_in0"]
    w_kpe = w_in0[:, 2 * MLA_Q_RANK:mla_cols]
    out["w_in0"] = _place(C0_N, _rwkv_col_pieces(w_in0[:, mla_cols:])[:1] + [
        (C0_QA, w_in0[:, :2 * MLA_Q_RANK]), (C0_KPE, w_kpe), (C0_KPEROT, _rot_half_cols(w_kpe)),
    ] + _rwkv_col_pieces(w_in0[:, mla_cols:])[1:], bf16)
    d_qk = MLA_NOPE + MLA_ROPE
    wqb = p["mla_w_qb"].reshape(MLA_Q_RANK, MLA_HEADS, d_qk)
    nope = wqb[:, :, :MLA_NOPE]
    rope = wqb[:, :, MLA_NOPE:]
    pad = jnp.zeros((MLA_Q_RANK, MLA_HEADS, LANES - MLA_ROPE), f32)
    flat = lambda x: x.reshape(MLA_Q_RANK, -1)
    out["wq_all"] = jnp.concatenate(
        [flat(nope), flat(jnp.concatenate([rope, pad], -1)), flat(jnp.concatenate([_rot_half_cols(rope), pad], -1))],
        axis=1).astype(bf16)
    out["w_kv"] = jnp.concatenate([p["mla_w_uk"].reshape(MLA_KV_RANK, -1), p["mla_w_uv"].reshape(MLA_KV_RANK, -1)],
                                  axis=1).astype(bf16)
    out["w_uk_t"] = jnp.transpose(p["mla_w_uk"], (1, 2, 0)).astype(bf16)
    out["w_uv_h"] = jnp.transpose(p["mla_w_uv"], (1, 0, 2)).astype(bf16)
    mu = _place(C0_N, _rwkv_col_pieces(p["rwkv_mu"].reshape(1, -1)))
    zrows = lambda n: jnp.zeros((n, RWKV_W), f32)
    head_id = jnp.arange(RWKV_W, dtype=jnp.int32) // RWKV_HEAD
    out["rwkv"] = {
        "mu_r": mu[:, C0_R:C0_R + RWKV_W], "mu_k": mu[:, C0_K:C0_K + RWKV_W], "mu_v": mu[:, C0_V:C0_V + RWKV_W],
        "mu_wdad": mu[:, C0_WDAD:C0_WDAD + LANES], "mu_gd": mu[:, C0_GD:C0_GD + 2 * LANES],
        "w0": p["rwkv_w0"].reshape(1, -1), "a0": p["rwkv_a0"].reshape(1, -1),
        "w2p": jnp.concatenate([p["rwkv_w2"], zrows(LANES - RWKV_DECAY_RANK)], 0).astype(bf16),
        "a2p": jnp.concatenate([zrows(RWKV_DECAY_RANK), p["rwkv_a2"]], 0).astype(bf16),
        "g2p": jnp.concatenate([p["rwkv_g2"], zrows(2 * LANES - RWKV_GATE_RANK)], 0).astype(bf16),
        "k_k": p["rwkv_k_k"].reshape(1, -1), "k_a": p["rwkv_k_a"].reshape(1, -1),
        "head_ones": (head_id[:, None] == head_id[None, :]).astype(bf16),
        "r_k": p["rwkv_r_k"].reshape(1, -1), "ln_w": p["rwkv_ln_w"].reshape(1, -1), "ln_b": p["rwkv_ln_b"].reshape(1, -1),
    }
    w_in1 = p["w_in1"]
    ggd0 = 3 * SB_W + 2 * GLA_HEADS * GLA_DK + GLA_HEADS * GLA_DV
    out["w_in1"] = _place(C1_N, [(0, w_in1[:, :ggd0]), (C1_GOG, w_in1[:, ggd0 + GLA_GATE_RANK:]),
                                 (C1_GGD, w_in1[:, ggd0:ggd0 + GLA_GATE_RANK])], bf16)
    out["gla"] = {
        "wgup": jnp.concatenate([p["gla_w_gup"], jnp.zeros((LANES - GLA_GATE_RANK, GLA_HEADS * GLA_DK), f32)],
                                0).astype(bf16),
        "bg": p["gla_b_g"].reshape(1, -1), "gn": p["gla_norm"].reshape(1, -1),
    }
    for name in ("w_out0", "w_out1", "ffn_w_gate", "ffn_w_up", "ffn_w_down"):
        out[name] = p[name].astype(bf16)
    return out


def _rope_tables(pos):
    half = MLA_ROPE // 2
    inv = 1.0 / (ROPE_THETA ** (jnp.arange(half, dtype=f32) / half))
    ang = pos.astype(f32)[:, None] * inv[None, :]
    cos, sin = jnp.cos(ang), jnp.sin(ang)
    return jnp.concatenate([cos] * 4, axis=1), jnp.concatenate([sin] * 4, axis=1)


def _rwkv_shift_row(cols0_row):
    return jnp.concatenate([cols0_row[..., C0_R:C0_R + 3 * RWKV_W], cols0_row[..., C0_WDAD:C0_WDAD + LANES],
                            cols0_row[..., C0_GD:C0_GD + RWKV_GATE_RANK]], axis=-1)


def _trunk(x, pos, past, p, w, batch, seq, tag):
    m = batch * seq
    cos, sin = _rope_tables(pos)
    cos, sin = jnp.tile(cos, (batch, 1)), jnp.tile(sin, (batch, 1))
    state = {}
    tmm = 1024 if m >= 1024 else m

    cols0 = matmul([(x, x.shape[1], 0)], [w["w_in0"]], gain=p["norm_mix"][0], tm=tmm, name=f"{tag}_in0")
    if past is None:
        qcat, kcat, v, ckv, kpe = mla_prep(cols0, cos, sin, p["mla_q_norm"], w["wq_all"], p["mla_kv_norm"],
                                           w["w_kv"], absorbed=False, name=f"{tag}_mla_prep")
        a_out = mla_flash(qcat, kcat, v, batch, seq, name=f"{tag}_mla_attn")
        shift_p = jnp.zeros((batch, C0_N), f32)
        s0 = jnp.zeros((batch, RWKV_HEADS, RWKV_HEAD, RWKV_HEAD), f32)
        gates = rwkv_gates(cols0, shift_p, seq, w["rwkv"], name=f"{tag}_rwkv_gates")
        b_out, s_rwkv = rwkv_scan(gates, s0, w["rwkv"], batch, seq, name=f"{tag}_rwkv_scan")
    else:
        qcat, ckv, kpe, knew = mla_prep(cols0, cos, sin, p["mla_q_norm"], w["wq_all"], p["mla_kv_norm"],
                                        w["w_uk_t"], absorbed=True, name=f"{tag}_mla_prep")
        lat = mla_decode(qcat, knew, past["mla_ckv"], past["mla_kpe"], past["page_table"], name=f"{tag}_mla_attn")
        a_out = head_matmul(lat, w["w_uv_h"], name=f"{tag}_mla_uv")
        shift_p = _place(C0_N, _rwkv_col_pieces(past["rwkv_shift"]))
        gates = rwkv_gates(cols0, shift_p, 1, w["rwkv"], name=f"{tag}_rwkv_gates")
        b_out, s_rwkv = rwkv_step(gates, past["rwkv_state"], w["rwkv"], name=f"{tag}_rwkv_step")
    state.update(mla_ckv=ckv.reshape(batch, seq, MLA_KV_RANK), mla_kpe=kpe.reshape(batch, seq, MLA_ROPE),
                 rwkv_state=s_rwkv, rwkv_shift=_rwkv_shift_row(cols0.reshape(batch, seq, C0_N)[:, -1]))
    half = w["w_out0"].shape[0] // 2
    x = matmul([(a_out, half, 0), (b_out, half, 0)], [w["w_out0"][:half], w["w_out0"][half:]], res=x, tm=tmm,
               name=f"{tag}_out0")
    x = ffn(x, p["norm_ffn"][0], w["ffn_w_gate"][0], w["ffn_w_up"][0], w["ffn_w_down"][0], name=f"{tag}_ffn0")

    cols1 = matmul([(x, x.shape[1], 0)], [w["w_in1"]], gain=p["norm_mix"][1], tm=tmm, name=f"{tag}_in1")
    if past is None:
        c_out = sb_flash(cols1, batch, seq, name=f"{tag}_sb_attn")
        g0 = jnp.zeros((batch, GLA_HEADS, GLA_DK, GLA_DV), f32)
        d_out, s_gla = gla_scan(cols1, g0, w["gla"], batch, seq, name=f"{tag}_gla_scan")
    else:
        c_out = sb_decode(cols1[:, C1_SQ:C1_SQ + SB_W], past["sb_k"], past["sb_v"], past["page_table"],
                          name=f"{tag}_sb_attn")
        la = gla_gate(cols1, w["gla"], name=f"{tag}_gla_gate")
        kw, vw = GLA_HEADS * GLA_DK, GLA_HEADS * GLA_DV
        d_out, s_gla = gla_step(cols1[:, C1_GQ:C1_GQ + kw], cols1[:, C1_GK:C1_GK + kw], la,
                                cols1[:, C1_GV:C1_GV + vw], cols1[:, C1_GOG:C1_GOG + vw], past["gla_state"],
                                w["gla"], name=f"{tag}_gla_step")
    state.update(sb_k=cols1[:, C1_SK:C1_SK + SB_W].reshape(batch, seq, SB_HEADS, SB_HEAD),
                 sb_v=cols1[:, C1_SV:C1_SV + SB_W].reshape(batch, seq, SB_HEADS, SB_HEAD), gla_state=s_gla)
    half = w["w_out1"].shape[0] // 2
    x = matmul([(c_out, half, 0), (d_out, half, 0)], [w["w_out1"][:half], w["w_out1"][half:]], res=x, tm=tmm,
               name=f"{tag}_out1")
    y = ffn(x, p["norm_ffn"][1], w["ffn_w_gate"][1], w["ffn_w_up"][1], w["ffn_w_down"][1],
            final_gain=p["norm_final"], name=f"{tag}_ffn1")
    return y.reshape(batch, seq, -1), state


def kernel(x_prompt, x_sample, cache_mla_ckv, cache_mla_kpe, cache_sb_k, cache_sb_v, state_rwkv, state_rwkv_shift, state_gla, page_table, w_in0, mla_q_norm, mla_w_qb, mla_kv_norm, mla_w_uk, mla_w_uv, rwkv_mu, rwkv_w0, rwkv_w2, rwkv_a0, rwkv_a2, rwkv_g2, rwkv_k_k, rwkv_k_a, rwkv_r_k, rwkv_ln_w, rwkv_ln_b, w_out0, w_in1, gla_w_gup, gla_b_g, gla_norm, w_out1, norm_mix, norm_ffn, ffn_w_gate, ffn_w_up, ffn_w_down, norm_final):
    p = dict(w_in0=w_in0, mla_q_norm=mla_q_norm, mla_w_qb=mla_w_qb, mla_kv_norm=mla_kv_norm, mla_w_uk=mla_w_uk,
             mla_w_uv=mla_w_uv, rwkv_mu=rwkv_mu, rwkv_w0=rwkv_w0, rwkv_w2=rwkv_w2, rwkv_a0=rwkv_a0,
             rwkv_a2=rwkv_a2, rwkv_g2=rwkv_g2, rwkv_k_k=rwkv_k_k, rwkv_k_a=rwkv_k_a, rwkv_r_k=rwkv_r_k,
             rwkv_ln_w=rwkv_ln_w, rwkv_ln_b=rwkv_ln_b, w_out0=w_out0, w_in1=w_in1, gla_w_gup=gla_w_gup,
             gla_b_g=gla_b_g, gla_norm=gla_norm, w_out1=w_out1, norm_mix=norm_mix, norm_ffn=norm_ffn,
             ffn_w_gate=ffn_w_gate, ffn_w_up=ffn_w_up, ffn_w_down=ffn_w_down, norm_final=norm_final)
    w = _prepare(p)
    b, t, d = x_prompt.shape
    db, dt, _ = x_sample.shape
    assert dt == 1
    past_len = page_table.shape[1] * PAGE_SIZE
    y_p, sp = _trunk(x_prompt.reshape(b * t, d), jnp.arange(t, dtype=jnp.int32), None, p, w, b, t, "p")
    past = dict(mla_ckv=cache_mla_ckv, mla_kpe=cache_mla_kpe, sb_k=cache_sb_k, sb_v=cache_sb_v,
                rwkv_state=state_rwkv, rwkv_shift=state_rwkv_shift, gla_state=state_gla, page_table=page_table)
    y_s, ss = _trunk(x_sample.reshape(db, d), past_len + jnp.arange(1, dtype=jnp.int32), past, p, w, db, 1, "s")
    keys = ("mla_ckv", "mla_kpe", "rwkv_state", "rwkv_shift", "sb_k", "sb_v", "gla_state")
    return (y_p, y_s) + tuple(sp[k] for k in keys) + tuple(ss[k] for k in keys)
```

```python
import functools
import math

import jax
import jax.numpy as jnp
from jax import lax
from jax.experimental import pallas as pl
from jax.experimental.pallas import tpu as pltpu

f32 = jnp.float32
bf16 = jnp.bfloat16

PAGE_SIZE = 128
MLA_HEADS = 8
MLA_NOPE = 128
MLA_ROPE = 64
MLA_V = 128
MLA_Q_RANK = 512
MLA_KV_RANK = 512
MLA_SCALE = (MLA_NOPE + MLA_ROPE) ** -0.5
ROPE_THETA = 10000.0
RWKV_HEADS = 16
RWKV_HEAD = 64
RWKV_W = RWKV_HEADS * RWKV_HEAD
RWKV_DECAY_RANK = 64
RWKV_A_RANK = 64
RWKV_GATE_RANK = 160
RWKV_GN_EPS = 64e-5
RWKV_COLS = 3 * RWKV_W + RWKV_DECAY_RANK + RWKV_A_RANK + RWKV_GATE_RANK
SB_HEADS = 16
SB_HEAD = 64
SB_W = SB_HEADS * SB_HEAD
SB_SCALE = SB_HEAD ** -0.5
GLA_HEADS = 4
GLA_DK = 128
GLA_DV = 256
GLA_GATE_RANK = 16
GLA_GATE_NORM = 16.0
NORM_EPS = 1e-6

LANES = 128
MXU_TILE = 256
VMEM_LIMIT_BYTES = 56 * 1024 * 1024

C0_R, C0_K, C0_V = 0, 1024, 2048
C0_QA, C0_KVA = 3072, 3584
C0_KPE, C0_KPEROT = 4096, 4224
C0_WDAD = 4352
C0_GD = 4608
C0_N = 5120
C1_SQ, C1_SK, C1_SV = 0, 1024, 2048
C1_GQ, C1_GK, C1_GV = 3072, 3584, 4096
C1_GOG = 5120
C1_GGD = 6144
C1_N = 6656

RWKV_CHUNK = 64
GLA_CHUNK = 64
MLA_SLOT = 2 * LANES
ABS_SLOT = MLA_KV_RANK + LANES


def _cparams(*sem):
    return pltpu.CompilerParams(dimension_semantics=sem, vmem_limit_bytes=VMEM_LIMIT_BYTES)


def _dot(a, b):
    return jnp.dot(a, b, preferred_element_type=f32)


def _dot_nt(a, b):
    return lax.dot_general(a, b, (((1,), (1,)), ((), ())), preferred_element_type=f32)


def _dot_tn(a, b):
    return lax.dot_general(a, b, (((0,), (0,)), ((), ())), preferred_element_type=f32)


def _split_dot_right(x, m):
    hi = x.astype(bf16)
    lo = (x - hi.astype(f32)).astype(bf16)
    return _dot(hi, m) + _dot(lo, m)


def _split_dot_left(m, x):
    hi = x.astype(bf16)
    lo = (x - hi.astype(f32)).astype(bf16)
    return _dot(m, hi) + _dot(m, lo)


def _sigmoid(x):
    return 1.0 / (1.0 + jnp.exp(-x))


def _softplus(x):
    return jnp.maximum(x, 0.0) + jnp.log(1.0 + jnp.exp(-jnp.abs(x)))


def _rms(x, gain):
    return x * lax.rsqrt(jnp.mean(x * x, axis=-1, keepdims=True) + NORM_EPS) * gain


def _mm_body(*refs, n_lhs, norm, res):
    lhs = refs[:n_lhs]
    ws = refs[n_lhs:2 * n_lhs]
    pos = 2 * n_lhs
    g_ref = refs[pos] if norm else None
    pos += int(norm)
    r_ref = refs[pos] if res else None
    pos += int(res)
    o_ref = refs[pos]
    hs = refs[pos + 1:pos + 1 + n_lhs]

    @pl.when(pl.program_id(1) == 0)
    def _():
        for i in range(n_lhs):
            x = lhs[i][...].astype(f32)
            if norm and i == 0:
                x = _rms(x, g_ref[...])
            hs[i][...] = x.astype(bf16)

    acc = _dot(hs[0][...], ws[0][...])
    for i in range(1, n_lhs):
        acc = acc + _dot(hs[i][...], ws[i][...])
    if res:
        acc = acc + r_ref[...]
    o_ref[...] = acc.astype(o_ref.dtype)


def matmul(lhs_list, w_list, *, gain=None, res=None, out_dtype=f32, tm=512, tn=512, name):
    m = lhs_list[0][0].shape[0]
    n = w_list[0].shape[1]
    tm, tn = min(tm, m), min(tn, n)
    assert m % tm == 0 and n % tn == 0
    in_specs, args, scratch = [], [], []
    for arr, k, cb in lhs_list:
        in_specs.append(pl.BlockSpec((tm, k), lambda i, j, cb=cb: (i, cb)))
        args.append(arr)
        scratch.append(pltpu.VMEM((tm, k), bf16))
    for (arr, k, cb), w in zip(lhs_list, w_list):
        assert w.shape[0] == k
        in_specs.append(pl.BlockSpec((k, tn), lambda i, j: (0, j)))
        args.append(w)
    if gain is not None:
        in_specs.append(pl.BlockSpec((1, lhs_list[0][1]), lambda i, j: (0, 0)))
        args.append(gain.reshape(1, -1))
    if res is not None:
        in_specs.append(pl.BlockSpec((tm, tn), lambda i, j: (i, j)))
        args.append(res)
    body = functools.partial(_mm_body, n_lhs=len(lhs_list), norm=gain is not None, res=res is not None)
    return pl.pallas_call(
        body,
        out_shape=jax.ShapeDtypeStruct((m, n), out_dtype),
        grid=(m // tm, n // tn),
        in_specs=in_specs,
        out_specs=pl.BlockSpec((tm, tn), lambda i, j: (i, j)),
        scratch_shapes=scratch,
        compiler_params=_cparams("parallel", "arbitrary"),
        name=name,
    )(*args)


def _hmm_body(x_ref, w_ref, o_ref):
    o_ref[...] = _dot(x_ref[...].astype(bf16), w_ref[0]).astype(o_ref.dtype)


def head_matmul(x, w, *, name):
    m = x.shape[0]
    h, k, n = w.shape
    return pl.pallas_call(
        _hmm_body,
        out_shape=jax.ShapeDtypeStruct((m, h * n), f32),
        grid=(h,),
        in_specs=[pl.BlockSpec((m, k), lambda i: (0, i)), pl.BlockSpec((1, k, n), lambda i: (i, 0, 0))],
        out_specs=pl.BlockSpec((m, n), lambda i: (0, i)),
        compiler_params=_cparams("arbitrary"),
        name=name,
    )(x, w)


def _ffn_body(*refs, final):
    x_ref, g_ref, wg_ref, wu_ref, wd_ref = refs[:5]
    gf_ref = refs[5] if final else None
    o_ref, h_ref, acc_ref = refs[5 + int(final):]
    f = pl.program_id(1)

    @pl.when(f == 0)
    def _():
        h_ref[...] = _rms(x_ref[...], g_ref[...]).astype(bf16)
        acc_ref[...] = jnp.zeros_like(acc_ref)

    h = h_ref[...]
    a = _dot(h, wg_ref[...])
    u = _dot(h, wu_ref[...])
    s = (a * _sigmoid(a) * u).astype(bf16)
    acc_ref[...] += _dot(s, wd_ref[...])

    @pl.when(f == pl.num_programs(1) - 1)
    def _():
        y = x_ref[...] + acc_ref[...]
        if final:
            y = _rms(y, gf_ref[...])
        o_ref[...] = y


def ffn(x, gain, wg, wu, wd, *, final_gain=None, tm=512, tf=512, name):
    m, d = x.shape
    dff = wg.shape[1]
    tm = min(tm, m)
    assert m % tm == 0 and dff % tf == 0
    in_specs = [
        pl.BlockSpec((tm, d), lambda i, j: (i, 0)),
        pl.BlockSpec((1, d), lambda i, j: (0, 0)),
        pl.BlockSpec((d, tf), lambda i, j: (0, j)),
        pl.BlockSpec((d, tf), lambda i, j: (0, j)),
        pl.BlockSpec((tf, d), lambda i, j: (j, 0)),
    ]
    args = [x, gain.reshape(1, d), wg, wu, wd]
    if final_gain is not None:
        in_specs.append(pl.BlockSpec((1, d), lambda i, j: (0, 0)))
        args.append(final_gain.reshape(1, d))
    return pl.pallas_call(
        functools.partial(_ffn_body, final=final_gain is not None),
        out_shape=jax.ShapeDtypeStruct((m, d), f32),
        grid=(m // tm, dff // tf),
        in_specs=in_specs,
        out_specs=pl.BlockSpec((tm, d), lambda i, j: (i, 0)),
        scratch_shapes=[pltpu.VMEM((tm, d), bf16), pltpu.VMEM((tm, d), f32)],
        compiler_params=_cparams("parallel", "arbitrary"),
        name=name,
    )(*args)


def _mla_prep_body(*refs, absorbed):
    (qa_ref, kva_ref, kpe_ref, kperot_ref, cos_ref, sin_ref, qg_ref, wq_ref, kg_ref, wkv_ref) = refs[:10]
    outs = refs[10:]
    cos, sin = cos_ref[...], sin_ref[...]
    qn = _rms(qa_ref[...], qg_ref[...]).astype(bf16)
    qall = _dot(qn, wq_ref[...])
    hw = MLA_HEADS * LANES
    nope, pe, rot = qall[:, :hw], qall[:, hw:2 * hw], qall[:, 2 * hw:]
    cos8 = jnp.concatenate([cos] * MLA_HEADS, axis=1)
    sin8 = jnp.concatenate([sin] * MLA_HEADS, axis=1)
    roped = pe * cos8 + rot * sin8
    ckv = _rms(kva_ref[...], kg_ref[...])
    kpe = kpe_ref[...] * cos + kperot_ref[...] * sin
    if absorbed:
        qcat_ref, ckv_ref, kpe_out_ref, knew_ref = outs
        pieces = []
        for h in range(MLA_HEADS):
            qabs = _dot(nope[:, h * LANES:(h + 1) * LANES].astype(bf16), wkv_ref[h])
            pieces += [qabs, roped[:, h * LANES:(h + 1) * LANES]]
        qcat_ref[...] = jnp.concatenate(pieces, axis=1).astype(bf16)
        knew_ref[...] = jnp.concatenate([ckv, kpe], axis=1)
    else:
        qcat_ref, kcat_ref, v_ref, ckv_ref, kpe_out_ref = outs
        kv = _dot(ckv.astype(bf16), wkv_ref[...])
        qp, kp = [], []
        for h in range(MLA_HEADS):
            sl = slice(h * LANES, (h + 1) * LANES)
            qp += [nope[:, sl], roped[:, sl]]
            kp += [kv[:, sl], kpe]
        qcat_ref[...] = jnp.concatenate(qp, axis=1).astype(bf16)
        kcat_ref[...] = jnp.concatenate(kp, axis=1).astype(bf16)
        v_ref[...] = kv[:, hw:].astype(bf16)
    ckv_ref[...] = ckv
    kpe_out_ref[...] = kpe[:, :MLA_ROPE]


def mla_prep(cols0, cos, sin, q_gain, wq_all, kv_gain, wkv, *, absorbed, tm=256, name):
    m = cols0.shape[0]
    tm = min(tm, m)
    assert m % tm == 0
    row = lambda w, cb: pl.BlockSpec((tm, w), lambda i, cb=cb: (i, cb))
    full = lambda a: pl.BlockSpec(a.shape, lambda i, nd=a.ndim: (0,) * nd)
    qg, kg = q_gain.reshape(1, -1), kv_gain.reshape(1, -1)
    in_specs = [
        row(MLA_Q_RANK, C0_QA // MLA_Q_RANK), row(MLA_KV_RANK, C0_KVA // MLA_KV_RANK),
        row(LANES, C0_KPE // LANES), row(LANES, C0_KPEROT // LANES),
        row(LANES, 0), row(LANES, 0), full(qg), full(wq_all), full(kg), full(wkv),
    ]
    if absorbed:
        out_shape = [jax.ShapeDtypeStruct((m, MLA_HEADS * ABS_SLOT), bf16),
                     jax.ShapeDtypeStruct((m, MLA_KV_RANK), f32),
                     jax.ShapeDtypeStruct((m, MLA_ROPE), f32),
                     jax.ShapeDtypeStruct((m, ABS_SLOT), f32)]
        out_specs = [row(MLA_HEADS * ABS_SLOT, 0), row(MLA_KV_RANK, 0), row(MLA_ROPE, 0), row(ABS_SLOT, 0)]
    else:
        out_shape = [jax.ShapeDtypeStruct((m, MLA_HEADS * MLA_SLOT), bf16),
                     jax.ShapeDtypeStruct((m, MLA_HEADS * MLA_SLOT), bf16),
                     jax.ShapeDtypeStruct((m, MLA_HEADS * MLA_V), bf16),
                     jax.ShapeDtypeStruct((m, MLA_KV_RANK), f32),
                     jax.ShapeDtypeStruct((m, MLA_ROPE), f32)]
        out_specs = [row(MLA_HEADS * MLA_SLOT, 0), row(MLA_HEADS * MLA_SLOT, 0), row(MLA_HEADS * MLA_V, 0),
                     row(MLA_KV_RANK, 0), row(MLA_ROPE, 0)]
    return pl.pallas_call(
        functools.partial(_mla_prep_body, absorbed=absorbed),
        out_shape=out_shape,
        grid=(m // tm,),
        in_specs=in_specs,
        out_specs=out_specs,
        compiler_params=_cparams("parallel"),
        name=name,
    )(cols0, cols0, cols0, cols0, cos, sin, qg, wq_all, kg, wkv)


def _causal_pairs(nblk, reverse):
    qi, kj = [], []
    for i in range(nblk):
        ks = range(i, -1, -1) if reverse else range(i + 1)
        for j in ks:
            qi.append(i)
            kj.append(j)
    return jnp.asarray(qi, jnp.int32), jnp.asarray(kj, jnp.int32)


def _mla_flash_body(qi_ref, kj_ref, q_ref, k_ref, v_ref, o_ref, m_ref, l_ref, acc_ref, *, hp):
    p = pl.program_id(2)
    i, j = qi_ref[p], kj_ref[p]
    heads = range(hp)

    @pl.when(j == 0)
    def _():
        m_ref[...] = jnp.full_like(m_ref, -jnp.inf)
        l_ref[...] = jnp.zeros_like(l_ref)
        acc_ref[...] = jnp.zeros_like(acc_ref)

    def step(diagonal):
        q, k, v = q_ref[...], k_ref[...], v_ref[...]
        qk = [slice(h * MLA_SLOT, (h + 1) * MLA_SLOT) for h in heads]
        s = [_dot_nt(q[:, qk[h]], k[:, qk[h]]) * MLA_SCALE for h in heads]
        if diagonal:
            keep = (lax.broadcasted_iota(jnp.int32, s[0].shape, 1) <= lax.broadcasted_iota(jnp.int32, s[0].shape, 0))
            s = [jnp.where(keep, x, -jnp.inf) for x in s]
        m_prev = [m_ref[h] for h in heads]
        m_new = [jnp.maximum(m_prev[h], jnp.max(s[h], axis=-1, keepdims=True)) for h in heads]
        alpha = [jnp.exp(m_prev[h] - m_new[h]) for h in heads]
        pr = [jnp.exp(s[h] - m_new[h]) for h in heads]
        for h in heads:
            l_ref[h] = l_ref[h] * alpha[h] + jnp.sum(pr[h], axis=-1, keepdims=True)
            acc_ref[h] = acc_ref[h] * alpha[h] + _dot(pr[h].astype(bf16), v[:, h * MLA_V:(h + 1) * MLA_V])
            m_ref[h] = m_new[h]

    @pl.when(j < i)
    def _():
        step(False)

    @pl.when(j == i)
    def _():
        step(True)
        o_ref[...] = jnp.concatenate([acc_ref[h] / l_ref[h] for h in heads], axis=1)


def mla_flash(qcat, kcat, v, batch, seq, *, tq=512, hp=4, name):
    tq = min(tq, seq)
    nblk = seq // tq
    qi, kj = _causal_pairs(nblk, reverse=False)
    grid_spec = pltpu.PrefetchScalarGridSpec(
        num_scalar_prefetch=2,
        grid=(batch, MLA_HEADS // hp, int(qi.shape[0])),
        in_specs=[
            pl.BlockSpec((tq, hp * MLA_SLOT), lambda b, h, p, qi, kj: (b * nblk + qi[p], h)),
            pl.BlockSpec((tq, hp * MLA_SLOT), lambda b, h, p, qi, kj: (b * nblk + kj[p], h)),
            pl.BlockSpec((tq, hp * MLA_V), lambda b, h, p, qi, kj: (b * nblk + kj[p], h)),
        ],
        out_specs=pl.BlockSpec((tq, hp * MLA_V), lambda b, h, p, qi, kj: (b * nblk + qi[p], h)),
        scratch_shapes=[pltpu.VMEM((hp, tq, 1), f32), pltpu.VMEM((hp, tq, 1), f32),
                        pltpu.VMEM((hp, tq, MLA_V), f32)],
    )
    return pl.pallas_call(
        functools.partial(_mla_flash_body, hp=hp),
        out_shape=jax.ShapeDtypeStruct((batch * seq, MLA_HEADS * MLA_V), f32),
        grid_spec=grid_spec,
        compiler_params=_cparams("parallel", "parallel", "arbitrary"),
        name=name,
    )(qi, kj, qcat, kcat, v)


def _sb_flash_body(qi_ref, kj_ref, q_ref, k_ref, v_ref, u_ref, o_ref, acc_ref, run_ref, *, tq, hp):
    p = pl.program_id(2)
    i, j = qi_ref[p], kj_ref[p]

    heads = range(hp)

    def step(diagonal):
        q = (q_ref[...] * SB_SCALE).astype(bf16)
        k, v = k_ref[...].astype(bf16), v_ref[...].astype(bf16)
        upper = u_ref[...]
        sls = [slice(h * SB_HEAD, (h + 1) * SB_HEAD) for h in heads]
        z = [_dot_nt(q[:, sl], k[:, sl]) for sl in sls]
        log_fail = [-_softplus(x) for x in z]
        if diagonal:
            valid = lax.broadcasted_iota(jnp.int32, z[0].shape, 1) < lax.broadcasted_iota(jnp.int32, z[0].shape, 0)
            log_fail = [jnp.where(valid, x, 0.0) for x in log_fail]
        sw = upper.shape[0]
        nsub = z[0].shape[1] // sw
        after, total = [], []
        for h in heads:
            parts = [log_fail[h][:, c * sw:(c + 1) * sw] for c in range(nsub)]
            sums = [jnp.sum(x, axis=-1, keepdims=True) for x in parts]
            local = [_split_dot_right(x, upper) for x in parts]
            later = sums[nsub - 1]
            for c in range(nsub - 2, -1, -1):
                local[c] = local[c] + later
                later = later + sums[c]
            after.append(jnp.concatenate(local, axis=1) if nsub > 1 else local[0])
            total.append(later)
        if diagonal:
            w = [jnp.where(valid, jnp.exp(z[h] + log_fail[h] + after[h]), 0.0) for h in heads]
        else:
            w = [jnp.exp(z[h] + log_fail[h] + after[h] + run_ref[h]) for h in heads]
        for h in heads:
            part = _dot(w[h].astype(bf16), v[:, sls[h]])
            if diagonal:
                acc_ref[h] = part
                run_ref[h] = total[h]
            else:
                acc_ref[h] += part
                run_ref[h] += total[h]

    @pl.when(j == i)
    def _():
        step(True)

    @pl.when(j < i)
    def _():
        step(False)

    @pl.when(j == 0)
    def _():
        o_ref[...] = jnp.concatenate([acc_ref[h] for h in heads], axis=1)


def sb_flash(cols1, batch, seq, *, tq=512, hp=4, name):
    tq = min(tq, seq)
    nblk = seq // tq
    qi, kj = _causal_pairs(nblk, reverse=True)
    w = hp * SB_HEAD
    nq, nk, nv = C1_SQ // w, C1_SK // w, C1_SV // w
    sw = min(tq, MXU_TILE)
    assert tq % sw == 0
    upper = (lax.broadcasted_iota(jnp.int32, (sw, sw), 0) > lax.broadcasted_iota(jnp.int32, (sw, sw), 1)).astype(bf16)
    grid_spec = pltpu.PrefetchScalarGridSpec(
        num_scalar_prefetch=2,
        grid=(batch, SB_HEADS // hp, int(qi.shape[0])),
        in_specs=[
            pl.BlockSpec((tq, w), lambda b, h, p, qi, kj: (b * nblk + qi[p], nq + h)),
            pl.BlockSpec((tq, w), lambda b, h, p, qi, kj: (b * nblk + kj[p], nk + h)),
            pl.BlockSpec((tq, w), lambda b, h, p, qi, kj: (b * nblk + kj[p], nv + h)),
            pl.BlockSpec((sw, sw), lambda b, h, p, qi, kj: (0, 0)),
        ],
        out_specs=pl.BlockSpec((tq, w), lambda b, h, p, qi, kj: (b * nblk + qi[p], h)),
        scratch_shapes=[pltpu.VMEM((hp, tq, SB_HEAD), f32), pltpu.VMEM((hp, tq, 1), f32)],
    )
    return pl.pallas_call(
        functools.partial(_sb_flash_body, tq=tq, hp=hp),
        out_shape=jax.ShapeDtypeStruct((batch * seq, SB_W), f32),
        grid_spec=grid_spec,
        compiler_params=_cparams("parallel", "parallel", "arbitrary"),
        name=name,
    )(qi, kj, cols1, cols1, cols1, upper)


def _rwkv_gates_body(*refs, seq_rows):
    (r_ref, k_ref, v_ref, wdad_ref, gd_ref) = refs[:5]
    pos = 5
    if seq_rows == 1:
        prevs = [ref[...] for ref in refs[pos:pos + 5]]
        pos += 5
    else:
        tails = refs[pos:pos + 5]
        firsts = refs[pos + 5:pos + 10]
        pos += 10
    (mu_r, mu_k, mu_v, mu_wdad, mu_gd, w0_ref, w2_ref, a0_ref, a2_ref, g2_ref, kk_ref, ka_ref, ones_ref) = refs[pos:pos + 13]
    (r_out, k_out, v_out, kkn_out, a_out, lw_out, g_out) = refs[pos + 13:]
    cur = [r_ref[...], k_ref[...], v_ref[...], wdad_ref[...], gd_ref[...]]
    if seq_rows != 1:
        tm = cur[0].shape[0]
        starts_seq = (pl.program_id(0) * tm) % seq_rows == 0
        prevs = []
        for x, tail, first in zip(cur, tails, firsts):
            carry = jnp.where(starts_seq, first[0], tail[7:8, :])
            rolled = pltpu.roll(x, 1, 0)
            rowid = lax.broadcasted_iota(jnp.int32, x.shape, 0)
            prevs.append(jnp.where(rowid == 0, carry, rolled))
    mus = [mu_r[...], mu_k[...], mu_v[...], mu_wdad[...], mu_gd[...]]
    xr, xk, xv, xwdad, xgd = [c + (p - c) * m for c, p, m in zip(cur, prevs, mus)]
    w_log = -_softplus(-(w0_ref[...] + _dot(jnp.tanh(xwdad).astype(bf16), w2_ref[...]))) - 0.5
    lw_out[...] = -jnp.exp(w_log)
    a = _sigmoid(a0_ref[...] + _dot(xwdad.astype(bf16), a2_ref[...]))
    g_out[...] = _dot(_sigmoid(xgd).astype(bf16), g2_ref[...])
    kk = xk * kk_ref[...]
    ssq = _split_dot_right(kk * kk, ones_ref[...])
    kkn_out[...] = kk / jnp.maximum(jnp.sqrt(ssq), 1e-12)
    k_out[...] = xk * (1.0 + (a - 1.0) * ka_ref[...])
    r_out[...] = xr
    v_out[...] = xv
    a_out[...] = a


def rwkv_gates(cols0, shift_p, seq_rows, prm, *, tm=256, name):
    m = cols0.shape[0]
    tm = min(tm, m, seq_rows) if seq_rows != 1 else min(tm, m)
    assert m % tm == 0 and (seq_rows == 1 or (seq_rows % tm == 0 and tm % 8 == 0))
    groups = [(RWKV_W, C0_R // RWKV_W), (RWKV_W, C0_K // RWKV_W), (RWKV_W, C0_V // RWKV_W),
              (LANES, C0_WDAD // LANES), (2 * LANES, C0_GD // (2 * LANES))]
    in_specs = [pl.BlockSpec((tm, w), lambda i, cb=cb: (i, cb)) for w, cb in groups]
    args = [cols0] * 5
    if seq_rows == 1:
        in_specs += [pl.BlockSpec((tm, w), lambda i, cb=cb: (i, cb)) for w, cb in groups]
        args += [shift_p] * 5
    else:
        per = tm // 8
        in_specs += [pl.BlockSpec((8, w), lambda i, cb=cb: (jnp.maximum(i * per - 1, 0), cb)) for w, cb in groups]
        args += [cols0] * 5
        shift3 = shift_p.reshape(shift_p.shape[0], 1, C0_N)
        in_specs += [pl.BlockSpec((1, 1, w), lambda i, cb=cb: ((i * tm) // seq_rows, 0, cb)) for w, cb in groups]
        args += [shift3] * 5
    small = [prm["mu_r"], prm["mu_k"], prm["mu_v"], prm["mu_wdad"], prm["mu_gd"], prm["w0"], prm["w2p"],
             prm["a0"], prm["a2p"], prm["g2p"], prm["k_k"], prm["k_a"], prm["head_ones"]]
    in_specs += [pl.BlockSpec(a.shape, lambda i: (0, 0)) for a in small]
    args += small
    out_spec = pl.BlockSpec((tm, RWKV_W), lambda i: (i, 0))
    return pl.pallas_call(
        functools.partial(_rwkv_gates_body, seq_rows=seq_rows),
        out_shape=[jax.ShapeDtypeStruct((m, RWKV_W), f32)] * 7,
        grid=(m // tm,),
        in_specs=in_specs,
        out_specs=[out_spec] * 7,
        compiler_params=_cparams("parallel"),
        name=name,
    )(*args)


def _rwkv_scan_body(r_ref, k_ref, v_ref, kk_ref, a_ref, lw_ref, g_ref, s0_ref, rk_ref, lnw_ref, lnb_ref,
                    y_ref, s_out_ref, s_ref):
    c = pl.program_id(1)
    C = r_ref.shape[0]

    @pl.when(c == 0)
    def _():
        s_ref[...] = s0_ref[0]

    rowi = lax.broadcasted_iota(jnp.int32, (C, C), 0)
    coli = lax.broadcasted_iota(jnp.int32, (C, C), 1)
    incl = coli <= rowi
    strict = coli < rowi
    lower = incl.astype(bf16)
    r, k, v, kk, a, lw, g = (x[...] for x in (r_ref, k_ref, v_ref, kk_ref, a_ref, lw_ref, g_ref))
    cum = _split_dot_left(lower, lw)
    gam = jnp.exp(cum)
    ginv = jnp.exp(-cum)
    a_m = -kk * jnp.exp(cum - lw)
    b_m = kk * a * ginv
    k_m = k * ginv
    r_m = r * gam
    cum_last = cum[C - 1:C, :]
    g_last = jnp.exp(cum_last)
    b_end = kk * a * jnp.exp(cum_last - cum)
    k_end = k * jnp.exp(cum_last - cum)
    bonus_w = r * k * rk_ref[...]
    lnw, lnb = lnw_ref[...], lnb_ref[...]
    heads = range(RWKV_HEADS)
    sls = [slice(h * RWKV_HEAD, (h + 1) * RWKV_HEAD) for h in heads]
    ar = [jnp.concatenate([a_m[:, sl], r_m[:, sl]], axis=0).astype(bf16) for sl in sls]
    vh = [v[:, sl].astype(bf16) for sl in sls]
    s_old = [s_ref[h] for h in heads]
    g_b = [_dot_nt(ar[h], b_m[:, sls[h]].astype(bf16)) for h in heads]
    g_k = [_dot_nt(ar[h], k_m[:, sls[h]].astype(bf16)) for h in heads]
    g_s = [_dot_nt(ar[h], s_old[h].astype(bf16)) for h in heads]
    x = [g_s[h][:C] + _dot(jnp.where(strict, g_k[h][:C], 0.0).astype(bf16), vh[h]) for h in heads]
    n = [jnp.where(strict, g_b[h][:C], 0.0).astype(bf16) for h in heads]
    steps = int(math.log2(C))
    for it in range(steps):
        x = [x[h] + _dot(n[h], x[h].astype(bf16)) for h in heads]
        if it + 1 < steps:
            n = [_dot(n[h], n[h]).astype(bf16) for h in heads]
    xb = [x[h].astype(bf16) for h in heads]
    y = [g_s[h][C:] + _dot(jnp.where(incl, g_b[h][C:], 0.0).astype(bf16), xb[h])
         + _dot(jnp.where(incl, g_k[h][C:], 0.0).astype(bf16), vh[h]) for h in heads]
    for h in heads:
        pv = jnp.concatenate([xb[h], vh[h]], axis=0)
        bk_end = jnp.concatenate([b_end[:, sls[h]], k_end[:, sls[h]]], axis=0).astype(bf16)
        s_ref[h] = s_old[h] * g_last[:, sls[h]] + _dot_tn(pv, bk_end)
    pieces = []
    for h in heads:
        sl = sls[h]
        mean = jnp.mean(y[h], axis=-1, keepdims=True)
        var = jnp.mean(jnp.square(y[h] - mean), axis=-1, keepdims=True)
        yn = (y[h] - mean) * lax.rsqrt(var + RWKV_GN_EPS) * lnw[:, sl] + lnb[:, sl]
        bonus = jnp.sum(bonus_w[:, sl], axis=-1, keepdims=True) * v[:, sl]
        pieces.append((yn + bonus) * g[:, sl])
    y_ref[...] = jnp.concatenate(pieces, axis=1)

    @pl.when(c == pl.num_programs(1) - 1)
    def _():
        s_out_ref[0] = s_ref[...]


def rwkv_scan(gates, s0, prm, batch, seq, *, name):
    C = min(RWKV_CHUNK, seq)
    nch = seq // C
    tok = pl.BlockSpec((C, RWKV_W), lambda b, c: (b * nch + c, 0))
    st = pl.BlockSpec((1, RWKV_HEADS, RWKV_HEAD, RWKV_HEAD), lambda b, c: (b, 0, 0, 0))
    vec = pl.BlockSpec((1, RWKV_W), lambda b, c: (0, 0))
    r, k, v, kkn, a, lw, g = gates
    return pl.pallas_call(
        _rwkv_scan_body,
        out_shape=[jax.ShapeDtypeStruct((batch * seq, RWKV_W), f32),
                   jax.ShapeDtypeStruct((batch, RWKV_HEADS, RWKV_HEAD, RWKV_HEAD), f32)],
        grid=(batch, nch),
        in_specs=[tok] * 7 + [st, vec, vec, vec],
        out_specs=[tok, st],
        scratch_shapes=[pltpu.VMEM((RWKV_HEADS, RWKV_HEAD, RWKV_HEAD), f32)],
        compiler_params=_cparams("parallel", "arbitrary"),
        name=name,
    )(r, k, v, kkn, a, lw, g, s0, prm["r_k"], prm["ln_w"], prm["ln_b"])


def _rwkv_step_body(r_ref, k_ref, v_ref, kk_ref, a_ref, lw_ref, g_ref, s0_ref, rk_ref, lnw_ref, lnb_ref,
                    y_ref, s_out_ref):
    n = RWKV_HEAD
    eye = (lax.broadcasted_iota(jnp.int32, (n, n), 0) == lax.broadcasted_iota(jnp.int32, (n, n), 1)).astype(f32)
    r, k, v, kk, a, lw, g = (x[...] for x in (r_ref, k_ref, v_ref, kk_ref, a_ref, lw_ref, g_ref))
    s = s0_ref[...]
    sa = jnp.sum(s * (-kk), axis=-1, keepdims=True)
    vcol = jnp.sum(eye * v, axis=-1, keepdims=True)
    s = s * jnp.exp(lw) + sa * (kk * a) + vcol * k
    s_out_ref[...] = s
    ycol = jnp.sum(s * r, axis=-1, keepdims=True)
    y = jnp.sum(eye * ycol, axis=-2, keepdims=True)
    mean = jnp.mean(y, axis=-1, keepdims=True)
    var = jnp.mean(jnp.square(y - mean), axis=-1, keepdims=True)
    y = (y - mean) * lax.rsqrt(var + RWKV_GN_EPS) * lnw_ref[...] + lnb_ref[...]
    bonus = jnp.sum(r * k * rk_ref[...], axis=-1, keepdims=True) * v
    y_ref[...] = (y + bonus) * g


def rwkv_step(gates, s0, prm, *, bs=8, name):
    m = s0.shape[0]
    bs = min(bs, m)
    assert m % bs == 0
    h4 = lambda x: x.reshape(-1, RWKV_HEADS, 1, RWKV_HEAD)
    vec = pl.BlockSpec((bs, RWKV_HEADS, 1, RWKV_HEAD), lambda i: (i, 0, 0, 0))
    st = pl.BlockSpec((bs, RWKV_HEADS, RWKV_HEAD, RWKV_HEAD), lambda i: (i, 0, 0, 0))
    par = pl.BlockSpec((1, RWKV_HEADS, 1, RWKV_HEAD), lambda i: (0, 0, 0, 0))
    y, s = pl.pallas_call(
        _rwkv_step_body,
        out_shape=[jax.ShapeDtypeStruct((m, RWKV_HEADS, 1, RWKV_HEAD), f32), jax.ShapeDtypeStruct(s0.shape, f32)],
        grid=(m // bs,),
        in_specs=[vec] * 7 + [st, par, par, par],
        out_specs=[vec, st],
        compiler_params=_cparams("parallel"),
        name=name,
    )(*[h4(x) for x in gates], s0, h4(prm["r_k"]), h4(prm["ln_w"]), h4(prm["ln_b"]))
    return y.reshape(m, RWKV_W), s


def _gla_log_gate(ggd, wgup, bg):
    x = _dot(ggd.astype(bf16), wgup) + bg
    return -_softplus(-x) / GLA_GATE_NORM


def _gla_out(o, gain, gog):
    return _rms(o, gain) * (gog * _sigmoid(gog))


def _gla_scan_body(q_ref, k_ref, v_ref, ggd_ref, gog_ref, s0_ref, wgup_ref, bg_ref, gn_ref,
                   d_ref, s_out_ref, s_ref):
    c = pl.program_id(1)
    C = q_ref.shape[0]

    @pl.when(c == 0)
    def _():
        s_ref[...] = s0_ref[0]

    rowi = lax.broadcasted_iota(jnp.int32, (C, C), 0)
    coli = lax.broadcasted_iota(jnp.int32, (C, C), 1)
    incl = coli <= rowi
    lower = incl.astype(bf16)
    eye = (lax.broadcasted_iota(jnp.int32, (GLA_DK, GLA_DK), 0)
           == lax.broadcasted_iota(jnp.int32, (GLA_DK, GLA_DK), 1)).astype(f32)
    la = _gla_log_gate(ggd_ref[...], wgup_ref[...], bg_ref[...])
    b = _split_dot_left(lower, la)
    mid = (C // 2) // 8 * 8
    b_mid = b[mid:mid + 1, :]
    b_last = b[C - 1:C, :]
    qs = q_ref[...] * GLA_DK ** -0.5
    k, v, gog = k_ref[...], v_ref[...], gog_ref[...]
    q_inter = (qs * jnp.exp(b)).astype(bf16)
    q_mid = (qs * jnp.exp(b - b_mid)).astype(bf16)
    k_mid = (k * jnp.exp(b_mid - b)).astype(bf16)
    k_end = (k * jnp.exp(b_last - b)).astype(bf16)
    e_last = jnp.exp(b_last)
    heads = range(GLA_HEADS)
    ks = [slice(h * GLA_DK, (h + 1) * GLA_DK) for h in heads]
    vs = [slice(h * GLA_DV, (h + 1) * GLA_DV) for h in heads]
    s_old = [s_ref[h] for h in heads]
    vh = [v[:, vs[h]].astype(bf16) for h in heads]
    att = [jnp.where(incl, _dot_nt(q_mid[:, ks[h]], k_mid[:, ks[h]]), 0.0).astype(bf16) for h in heads]
    o = [_dot(q_inter[:, ks[h]], s_old[h].astype(bf16)) + _dot(att[h], vh[h]) for h in heads]
    e_col = [jnp.sum(eye * e_last[:, ks[h]], axis=-1, keepdims=True) for h in heads]
    for h in heads:
        s_ref[h] = s_old[h] * e_col[h] + _dot_tn(k_end[:, ks[h]], vh[h])
    d_ref[...] = jnp.concatenate([_gla_out(o[h], gn_ref[...], gog[:, vs[h]]) for h in heads], axis=1)

    @pl.when(c == pl.num_programs(1) - 1)
    def _():
        s_out_ref[0] = s_ref[...]


def gla_scan(cols1, s0, prm, batch, seq, *, name):
    C = min(GLA_CHUNK, seq)
    nch = seq // C
    kw, vw = GLA_HEADS * GLA_DK, GLA_HEADS * GLA_DV
    tok = lambda w, off: pl.BlockSpec((C, w), lambda b, c, cb=off // w: (b * nch + c, cb))
    st = pl.BlockSpec((1, GLA_HEADS, GLA_DK, GLA_DV), lambda b, c: (b, 0, 0, 0))
    full = lambda a: pl.BlockSpec(a.shape, lambda b, c: (0, 0))
    return pl.pallas_call(
        _gla_scan_body,
        out_shape=[jax.ShapeDtypeStruct((batch * seq, vw), f32),
                   jax.ShapeDtypeStruct((batch, GLA_HEADS, GLA_DK, GLA_DV), f32)],
        grid=(batch, nch),
        in_specs=[tok(kw, C1_GQ), tok(kw, C1_GK), tok(vw, C1_GV), tok(LANES, C1_GGD), tok(vw, C1_GOG), st,
                  full(prm["wgup"]), full(prm["bg"]), full(prm["gn"])],
        out_specs=[pl.BlockSpec((C, vw), lambda b, c: (b * nch + c, 0)), st],
        scratch_shapes=[pltpu.VMEM((GLA_HEADS, GLA_DK, GLA_DV), f32)],
        compiler_params=_cparams("parallel", "arbitrary"),
        name=name,
    )(cols1, cols1, cols1, cols1, cols1, s0, prm["wgup"], prm["bg"], prm["gn"])


def _gla_gate_body(ggd_ref, wgup_ref, bg_ref, la_ref):
    la_ref[...] = _gla_log_gate(ggd_ref[...], wgup_ref[...], bg_ref[...])


def gla_gate(cols1, prm, *, name):
    m = cols1.shape[0]
    kw = GLA_HEADS * GLA_DK
    return pl.pallas_call(
        _gla_gate_body,
        out_shape=jax.ShapeDtypeStruct((m, kw), f32),
        grid=(1,),
        in_specs=[pl.BlockSpec((m, LANES), lambda i: (0, C1_GGD // LANES)),
                  pl.BlockSpec(prm["wgup"].shape, lambda i: (0, 0)), pl.BlockSpec(prm["bg"].shape, lambda i: (0, 0))],
        out_specs=pl.BlockSpec((m, kw), lambda i: (0, 0)),
        compiler_params=_cparams("arbitrary"),
        name=name,
    )(cols1, prm["wgup"], prm["bg"])


def _gla_step_body(q_ref, k_ref, la_ref, v_ref, gog_ref, s0_ref, gn_ref, d_ref, s_out_ref):
    n = GLA_DK
    eye = (lax.broadcasted_iota(jnp.int32, (n, n), 0) == lax.broadcasted_iota(jnp.int32, (n, n), 1)).astype(f32)
    col = lambda x: jnp.sum(eye * x, axis=-1, keepdims=True)
    qc = col(q_ref[...] * GLA_DK ** -0.5)
    kc = col(k_ref[...])
    ec = col(jnp.exp(la_ref[...]))
    s = s0_ref[...] * ec + kc * v_ref[...]
    s_out_ref[...] = s
    o = jnp.sum(qc * s, axis=-2, keepdims=True)
    d_ref[...] = _gla_out(o, gn_ref[...], gog_ref[...])


def gla_step(q, k, la, v, gog, s0, prm, *, bs=8, name):
    m = s0.shape[0]
    bs = min(bs, m)
    assert m % bs == 0
    hk = lambda x: x.reshape(m, GLA_HEADS, 1, GLA_DK)
    hv = lambda x: x.reshape(m, GLA_HEADS, 1, GLA_DV)
    ks = pl.BlockSpec((bs, GLA_HEADS, 1, GLA_DK), lambda i: (i, 0, 0, 0))
    vs = pl.BlockSpec((bs, GLA_HEADS, 1, GLA_DV), lambda i: (i, 0, 0, 0))
    st = pl.BlockSpec((bs, GLA_HEADS, GLA_DK, GLA_DV), lambda i: (i, 0, 0, 0))
    gn = prm["gn"].reshape(1, 1, 1, GLA_DV)
    d, s = pl.pallas_call(
        _gla_step_body,
        out_shape=[jax.ShapeDtypeStruct((m, GLA_HEADS, 1, GLA_DV), f32), jax.ShapeDtypeStruct(s0.shape, f32)],
        grid=(m // bs,),
        in_specs=[ks, ks, ks, vs, vs, st, pl.BlockSpec(gn.shape, lambda i: (0, 0, 0, 0))],
        out_specs=[vs, st],
        compiler_params=_cparams("parallel"),
        name=name,
    )(hk(q), hk(k), hk(la), hv(v), hv(gog), s0, gn)
    return d.reshape(m, GLA_HEADS * GLA_DV), s


def _mla_decode_body(*refs, pages):
    pt_ref, q_ref, knew_ref = refs[:3]
    ckv_refs = refs[3:3 + pages]
    kpe_refs = refs[3 + pages:3 + 2 * pages]
    o_ref, m_ref, l_ref, acc_ref = refs[3 + 2 * pages:]
    j = pl.program_id(1)
    q = q_ref[0]
    q_abs, q_pe = q[:, :MLA_KV_RANK], q[:, MLA_KV_RANK:MLA_KV_RANK + MLA_ROPE]

    groups = m_ref.shape[0]
    per = pages // groups

    @pl.when(j == 0)
    def _():
        kn = knew_ref[0].astype(bf16).astype(f32)
        s_new = jnp.sum(q.astype(f32) * kn, axis=-1, keepdims=True) * MLA_SCALE
        m_ref[...] = jnp.full_like(m_ref, -jnp.inf)
        l_ref[...] = jnp.zeros_like(l_ref)
        acc_ref[...] = jnp.zeros_like(acc_ref)
        m_ref[0] = s_new
        l_ref[0] = jnp.ones_like(s_new)
        acc_ref[0] = jnp.broadcast_to(kn[:, :MLA_KV_RANK], acc_ref.shape[1:])

    cks = [r[0].astype(bf16) for r in ckv_refs]
    scores = [(_dot_nt(q_abs, ck) + _dot(q_pe, kr[0].astype(bf16))) * MLA_SCALE
              for ck, kr in zip(cks, kpe_refs)]
    gs = range(groups)
    s = [jnp.concatenate(scores[g * per:(g + 1) * per], axis=1) for g in gs]
    m_prev = [m_ref[g] for g in gs]
    m_new = [jnp.maximum(m_prev[g], jnp.max(s[g], axis=-1, keepdims=True)) for g in gs]
    alpha = [jnp.exp(m_prev[g] - m_new[g]) for g in gs]
    pr = [jnp.exp(s[g] - m_new[g]) for g in gs]
    for g in gs:
        l_ref[g] = l_ref[g] * alpha[g] + jnp.sum(pr[g], axis=-1, keepdims=True)
        acc = acc_ref[g] * alpha[g]
        for i in range(per):
            acc = acc + _dot(pr[g][:, i * PAGE_SIZE:(i + 1) * PAGE_SIZE].astype(bf16), cks[g * per + i])
        acc_ref[g] = acc
        m_ref[g] = m_new[g]

    @pl.when(j == pl.num_programs(1) - 1)
    def _():
        m_all = m_ref[0]
        for g in range(1, groups):
            m_all = jnp.maximum(m_all, m_ref[g])
        scale = [jnp.exp(m_ref[g] - m_all) for g in gs]
        num = sum(acc_ref[g] * scale[g] for g in gs)
        den = sum(l_ref[g] * scale[g] for g in gs)
        o_ref[0] = num / den


def mla_decode(qcat, knew, cache_ckv, cache_kpe, page_table, *, pages=16, groups=2, name):
    nseq, npages = page_table.shape
    pages = min(pages, npages)
    groups = min(groups, pages)
    assert npages % pages == 0 and pages % groups == 0
    q3 = qcat.reshape(nseq, MLA_HEADS, ABS_SLOT)
    kn3 = knew.reshape(nseq, 1, ABS_SLOT)
    kpe_t = jnp.transpose(cache_kpe, (0, 2, 1))
    page = lambda r, c, i: pl.BlockSpec((1, r, c), lambda b, j, pt, i=i: (pt[b, j * pages + i], 0, 0))
    grid_spec = pltpu.PrefetchScalarGridSpec(
        num_scalar_prefetch=1,
        grid=(nseq, npages // pages),
        in_specs=[pl.BlockSpec((1, MLA_HEADS, ABS_SLOT), lambda b, j, pt: (b, 0, 0)),
                  pl.BlockSpec((1, 1, ABS_SLOT), lambda b, j, pt: (b, 0, 0))]
        + [page(PAGE_SIZE, MLA_KV_RANK, i) for i in range(pages)]
        + [page(MLA_ROPE, PAGE_SIZE, i) for i in range(pages)],
        out_specs=pl.BlockSpec((1, MLA_HEADS, MLA_KV_RANK), lambda b, j, pt: (b, 0, 0)),
        scratch_shapes=[pltpu.VMEM((groups, MLA_HEADS, 1), f32), pltpu.VMEM((groups, MLA_HEADS, 1), f32),
                        pltpu.VMEM((groups, MLA_HEADS, MLA_KV_RANK), f32)],
    )
    lat = pl.pallas_call(
        functools.partial(_mla_decode_body, pages=pages),
        out_shape=jax.ShapeDtypeStruct((nseq, MLA_HEADS, MLA_KV_RANK), f32),
        grid_spec=grid_spec,
        compiler_params=_cparams("parallel", "arbitrary"),
        name=name,
    )(page_table, q3, kn3, *([cache_ckv] * pages), *([kpe_t] * pages))
    return lat.reshape(nseq, MLA_HEADS * MLA_KV_RANK)


def _sb_decode_body(*refs, pages):
    pt_ref, q_ref, upper_ref = refs[:3]
    k_refs = refs[3:3 + pages]
    v_refs = refs[3 + pages:3 + 2 * pages]
    o_ref, qb_ref, acc_ref, run_ref = refs[3 + 2 * pages:]
    j = pl.program_id(1)
    n = SB_HEAD
    eye = (lax.broadcasted_iota(jnp.int32, (n, n), 0) == lax.broadcasted_iota(jnp.int32, (n, n), 1)).astype(f32)

    @pl.when(j == 0)
    def _():
        q = q_ref[0]
        qcol = jnp.sum(eye[None] * q[:, None, :], axis=-1, keepdims=True)
        qb_ref[...] = jnp.broadcast_to(qcol, qb_ref.shape)
        acc_ref[...] = jnp.zeros_like(acc_ref)
        run_ref[...] = jnp.zeros_like(run_ref)

    qb = qb_ref[...]
    z = jnp.concatenate([jnp.sum(qb * k_ref[0], axis=1) for k_ref in k_refs], axis=0) * SB_SCALE
    log_fail = -_softplus(z)
    after = _split_dot_right(log_fail, upper_ref[...])
    total = jnp.sum(log_fail, axis=-1, keepdims=True)
    run = run_ref[...]
    ws = []
    for p in range(pages):
        rows = slice(p * SB_HEADS, (p + 1) * SB_HEADS)
        ws.append(jnp.exp(z[rows] + log_fail[rows] + after[rows] + run))
        run = run + total[rows]
    run_ref[...] = run
    for h in range(SB_HEADS):
        a = acc_ref[h]
        for p in range(pages):
            a = a + v_refs[p][0, h] * ws[p][h:h + 1, :]
        acc_ref[h] = a

    @pl.when(j == pl.num_programs(1) - 1)
    def _():
        ocol = jnp.sum(acc_ref[...], axis=-1, keepdims=True)
        o_ref[0] = jnp.sum(eye[None] * ocol, axis=1)


def sb_decode(q, cache_k, cache_v, page_table, *, pages=8, name):
    nseq, npages = page_table.shape
    pages = min(pages, npages)
    assert npages % pages == 0
    ck = jnp.transpose(cache_k, (0, 2, 3, 1))
    cv = jnp.transpose(cache_v, (0, 2, 3, 1))
    q3 = q.reshape(nseq, SB_HEADS, SB_HEAD)
    upper = (lax.broadcasted_iota(jnp.int32, (PAGE_SIZE, PAGE_SIZE), 0)
             > lax.broadcasted_iota(jnp.int32, (PAGE_SIZE, PAGE_SIZE), 1)).astype(bf16)
    page = lambda i: pl.BlockSpec(
        (1, SB_HEADS, SB_HEAD, PAGE_SIZE), lambda b, j, pt, i=i: (pt[b, npages - 1 - (j * pages + i)], 0, 0, 0))
    state = pltpu.VMEM((SB_HEADS, SB_HEAD, PAGE_SIZE), f32)
    grid_spec = pltpu.PrefetchScalarGridSpec(
        num_scalar_prefetch=1,
        grid=(nseq, npages // pages),
        in_specs=[pl.BlockSpec((1, SB_HEADS, SB_HEAD), lambda b, j, pt: (b, 0, 0)),
                  pl.BlockSpec((PAGE_SIZE, PAGE_SIZE), lambda b, j, pt: (0, 0))]
        + [page(i) for i in range(pages)] * 2,
        out_specs=pl.BlockSpec((1, SB_HEADS, SB_HEAD), lambda b, j, pt: (b, 0, 0)),
        scratch_shapes=[state, state, pltpu.VMEM((SB_HEADS, 1), f32)],
    )
    out = pl.pallas_call(
        functools.partial(_sb_decode_body, pages=pages),
        out_shape=jax.ShapeDtypeStruct((nseq, SB_HEADS, SB_HEAD), f32),
        grid_spec=grid_spec,
        compiler_params=_cparams("parallel", "arbitrary"),
        name=name,
    )(page_table, q3, upper, *([ck] * pages), *([cv] * pages))
    return out.reshape(nseq, SB_W)


def _place(width, pieces, dtype=None):
    lead = pieces[0][1].shape[:-1]
    out, pos = [], 0
    for off, arr in pieces:
        if off > pos:
            out.append(jnp.zeros(lead + (off - pos,), arr.dtype))
        out.append(arr)
        pos = off + arr.shape[-1]
    if width > pos:
        out.append(jnp.zeros(lead + (width - pos,), pieces[0][1].dtype))
    res = jnp.concatenate(out, axis=-1)
    return res if dtype is None else res.astype(dtype)


def _rot_half_cols(w):
    half = w.shape[-1] // 2
    return jnp.concatenate([-w[..., half:], w[..., :half]], axis=-1)


def _rwkv_col_pieces(x):
    w = RWKV_W
    return [(C0_R, x[..., :3 * w]), (C0_WDAD, x[..., 3 * w:3 * w + 128]), (C0_GD, x[..., 3 * w + 128:])]


def _prepare(p):
    out = {}
    mla_cols = 2 * MLA_Q_RANK + MLA_ROPE
    w_in0 = p["w_in0"]
    w_kpe = w_in0[:, 2 * MLA_Q_RANK:mla_cols]
    out["w_in0"] = _place(C0_N, _rwkv_col_pieces(w_in0[:, mla_cols:])[:1] + [
        (C0_QA, w_in0[:, :2 * MLA_Q_RANK]), (C0_KPE, w_kpe), (C0_KPEROT, _rot_half_cols(w_kpe)),
    ] + _rwkv_col_pieces(w_in0[:, mla_cols:])[1:], bf16)
    d_qk = MLA_NOPE + MLA_ROPE
    wqb = p["mla_w_qb"].reshape(MLA_Q_RANK, MLA_HEADS, d_qk)
    nope = wqb[:, :, :MLA_NOPE]
    rope = wqb[:, :, MLA_NOPE:]
    pad = jnp.zeros((MLA_Q_RANK, MLA_HEADS, LANES - MLA_ROPE), f32)
    flat = lambda x: x.reshape(MLA_Q_RANK, -1)
    out["wq_all"] = jnp.concatenate(
        [flat(nope), flat(jnp.concatenate([rope, pad], -1)), flat(jnp.concatenate([_rot_half_cols(rope), pad], -1))],
        axis=1).astype(bf16)
    out["w_kv"] = jnp.concatenate([p["mla_w_uk"].reshape(MLA_KV_RANK, -1), p["mla_w_uv"].reshape(MLA_KV_RANK, -1)],
                                  axis=1).astype(bf16)
    out["w_uk_t"] = jnp.transpose(p["mla_w_uk"], (1, 2, 0)).astype(bf16)
    out["w_uv_h"] = jnp.transpose(p["mla_w_uv"], (1, 0, 2)).astype(bf16)
    mu = _place(C0_N, _rwkv_col_pieces(p["rwkv_mu"].reshape(1, -1)))
    zrows = lambda n: jnp.zeros((n, RWKV_W), f32)
    head_id = jnp.arange(RWKV_W, dtype=jnp.int32) // RWKV_HEAD
    out["rwkv"] = {
        "mu_r": mu[:, C0_R:C0_R + RWKV_W], "mu_k": mu[:, C0_K:C0_K + RWKV_W], "mu_v": mu[:, C0_V:C0_V + RWKV_W],
        "mu_wdad": mu[:, C0_WDAD:C0_WDAD + LANES], "mu_gd": mu[:, C0_GD:C0_GD + 2 * LANES],
        "w0": p["rwkv_w0"].reshape(1, -1), "a0": p["rwkv_a0"].reshape(1, -1),
        "w2p": jnp.concatenate([p["rwkv_w2"], zrows(LANES - RWKV_DECAY_RANK)], 0).astype(bf16),
        "a2p": jnp.concatenate([zrows(RWKV_DECAY_RANK), p["rwkv_a2"]], 0).astype(bf16),
        "g2p": jnp.concatenate([p["rwkv_g2"], zrows(2 * LANES - RWKV_GATE_RANK)], 0).astype(bf16),
        "k_k": p["rwkv_k_k"].reshape(1, -1), "k_a": p["rwkv_k_a"].reshape(1, -1),
        "head_ones": (head_id[:, None] == head_id[None, :]).astype(bf16),
        "r_k": p["rwkv_r_k"].reshape(1, -1), "ln_w": p["rwkv_ln_w"].reshape(1, -1), "ln_b": p["rwkv_ln_b"].reshape(1, -1),
    }
    w_in1 = p["w_in1"]
    ggd0 = 3 * SB_W + 2 * GLA_HEADS * GLA_DK + GLA_HEADS * GLA_DV
    out["w_in1"] = _place(C1_N, [(0, w_in1[:, :ggd0]), (C1_GOG, w_in1[:, ggd0 + GLA_GATE_RANK:]),
                                 (C1_GGD, w_in1[:, ggd0:ggd0 + GLA_GATE_RANK])], bf16)
    out["gla"] = {
        "wgup": jnp.concatenate([p["gla_w_gup"], jnp.zeros((LANES - GLA_GATE_RANK, GLA_HEADS * GLA_DK), f32)],
                                0).astype(bf16),
        "bg": p["gla_b_g"].reshape(1, -1), "gn": p["gla_norm"].reshape(1, -1),
    }
    for name in ("w_out0", "w_out1", "ffn_w_gate", "ffn_w_up", "ffn_w_down"):
        out[name] = p[name].astype(bf16)
    return out


def _rope_tables(pos):
    half = MLA_ROPE // 2
    inv = 1.0 / (ROPE_THETA ** (jnp.arange(half, dtype=f32) / half))
    ang = pos.astype(f32)[:, None] * inv[None, :]
    cos, sin = jnp.cos(ang), jnp.sin(ang)
    return jnp.concatenate([cos] * 4, axis=1), jnp.concatenate([sin] * 4, axis=1)


def _rwkv_shift_row(cols0_row):
    return jnp.concatenate([cols0_row[..., C0_R:C0_R + 3 * RWKV_W], cols0_row[..., C0_WDAD:C0_WDAD + LANES],
                            cols0_row[..., C0_GD:C0_GD + RWKV_GATE_RANK]], axis=-1)


def _trunk(x, pos, past, p, w, batch, seq, tag):
    m = batch * seq
    cos, sin = _rope_tables(pos)
    cos, sin = jnp.tile(cos, (batch, 1)), jnp.tile(sin, (batch, 1))
    state = {}
    tmm = 1024 if m >= 1024 else m

    cols0 = matmul([(x, x.shape[1], 0)], [w["w_in0"]], gain=p["norm_mix"][0], tm=tmm, name=f"{tag}_in0")
    if past is None:
        qcat, kcat, v, ckv, kpe = mla_prep(cols0, cos, sin, p["mla_q_norm"], w["wq_all"], p["mla_kv_norm"],
                                           w["w_kv"], absorbed=False, name=f"{tag}_mla_prep")
        a_out = mla_flash(qcat, kcat, v, batch, seq, name=f"{tag}_mla_attn")
        shift_p = jnp.zeros((batch, C0_N), f32)
        s0 = jnp.zeros((batch, RWKV_HEADS, RWKV_HEAD, RWKV_HEAD), f32)
        gates = rwkv_gates(cols0, shift_p, seq, w["rwkv"], name=f"{tag}_rwkv_gates")
        b_out, s_rwkv = rwkv_scan(gates, s0, w["rwkv"], batch, seq, name=f"{tag}_rwkv_scan")
    else:
        qcat, ckv, kpe, knew = mla_prep(cols0, cos, sin, p["mla_q_norm"], w["wq_all"], p["mla_kv_norm"],
                                        w["w_uk_t"], absorbed=True, name=f"{tag}_mla_prep")
        lat = mla_decode(qcat, knew, past["mla_ckv"], past["mla_kpe"], past["page_table"], name=f"{tag}_mla_attn")
        a_out = head_matmul(lat, w["w_uv_h"], name=f"{tag}_mla_uv")
        shift_p = _place(C0_N, _rwkv_col_pieces(past["rwkv_shift"]))
        gates = rwkv_gates(cols0, shift_p, 1, w["rwkv"], name=f"{tag}_rwkv_gates")
        b_out, s_rwkv = rwkv_step(gates, past["rwkv_state"], w["rwkv"], name=f"{tag}_rwkv_step")
    state.update(mla_ckv=ckv.reshape(batch, seq, MLA_KV_RANK), mla_kpe=kpe.reshape(batch, seq, MLA_ROPE),
                 rwkv_state=s_rwkv, rwkv_shift=_rwkv_shift_row(cols0.reshape(batch, seq, C0_N)[:, -1]))
    half = w["w_out0"].shape[0] // 2
    x = matmul([(a_out, half, 0), (b_out, half, 0)], [w["w_out0"][:half], w["w_out0"][half:]], res=x, tm=tmm,
               name=f"{tag}_out0")
    x = ffn(x, p["norm_ffn"][0], w["ffn_w_gate"][0], w["ffn_w_up"][0], w["ffn_w_down"][0], name=f"{tag}_ffn0")

    cols1 = matmul([(x, x.shape[1], 0)], [w["w_in1"]], gain=p["norm_mix"][1], tm=tmm, name=f"{tag}_in1")
    if past is None:
        c_out = sb_flash(cols1, batch, seq, name=f"{tag}_sb_attn")
        g0 = jnp.zeros((batch, GLA_HEADS, GLA_DK, GLA_DV), f32)
        d_out, s_gla = gla_scan(cols1, g0, w["gla"], batch, seq, name=f"{tag}_gla_scan")
    else:
        c_out = sb_decode(cols1[:, C1_SQ:C1_SQ + SB_W], past["sb_k"], past["sb_v"], past["page_table"],
                          name=f"{tag}_sb_attn")
        la = gla_gate(cols1, w["gla"], name=f"{tag}_gla_gate")
        kw, vw = GLA_HEADS * GLA_DK, GLA_HEADS * GLA_DV
        d_out, s_gla = gla_step(cols1[:, C1_GQ:C1_GQ + kw], cols1[:, C1_GK:C1_GK + kw], la,
                                cols1[:, C1_GV:C1_GV + vw], cols1[:, C1_GOG:C1_GOG + vw], past["gla_state"],
                                w["gla"], name=f"{tag}_gla_step")
    state.update(sb_k=cols1[:, C1_SK:C1_SK + SB_W].reshape(batch, seq, SB_HEADS, SB_HEAD),
                 sb_v=cols1[:, C1_SV:C1_SV + SB_W].reshape(batch, seq, SB_HEADS, SB_HEAD), gla_state=s_gla)
    half = w["w_out1"].shape[0] // 2
    x = matmul([(c_out, half, 0), (d_out, half, 0)], [w["w_out1"][:half], w["w_out1"][half:]], res=x, tm=tmm,
               name=f"{tag}_out1")
    y = ffn(x, p["norm_ffn"][1], w["ffn_w_gate"][1], w["ffn_w_up"][1], w["ffn_w_down"][1],
            final_gain=p["norm_final"], name=f"{tag}_ffn1")
    return y.reshape(batch, seq, -1), state


def kernel(x_prompt, x_sample, cache_mla_ckv, cache_mla_kpe, cache_sb_k, cache_sb_v, state_rwkv, state_rwkv_shift, state_gla, page_table, w_in0, mla_q_norm, mla_w_qb, mla_kv_norm, mla_w_uk, mla_w_uv, rwkv_mu, rwkv_w0, rwkv_w2, rwkv_a0, rwkv_a2, rwkv_g2, rwkv_k_k, rwkv_k_a, rwkv_r_k, rwkv_ln_w, rwkv_ln_b, w_out0, w_in1, gla_w_gup, gla_b_g, gla_norm, w_out1, norm_mix, norm_ffn, ffn_w_gate, ffn_w_up, ffn_w_down, norm_final):
    p = dict(w_in0=w_in0, mla_q_norm=mla_q_norm, mla_w_qb=mla_w_qb, mla_kv_norm=mla_kv_norm, mla_w_uk=mla_w_uk,
             mla_w_uv=mla_w_uv, rwkv_mu=rwkv_mu, rwkv_w0=rwkv_w0, rwkv_w2=rwkv_w2, rwkv_a0=rwkv_a0,
             rwkv_a2=rwkv_a2, rwkv_g2=rwkv_g2, rwkv_k_k=rwkv_k_k, rwkv_k_a=rwkv_k_a, rwkv_r_k=rwkv_r_k,
             rwkv_ln_w=rwkv_ln_w, rwkv_ln_b=rwkv_ln_b, w_out0=w_out0, w_in1=w_in1, gla_w_gup=gla_w_gup,
             gla_b_g=gla_b_g, gla_norm=gla_norm, w_out1=w_out1, norm_mix=norm_mix, norm_ffn=norm_ffn,
             ffn_w_gate=ffn_w_gate, ffn_w_up=ffn_w_up, ffn_w_down=ffn_w_down, norm_final=norm_final)
    w = _prepare(p)
    b, t, d = x_prompt.shape
    db, dt, _ = x_sample.shape
    assert dt == 1
    past_len = page_table.shape[1] * PAGE_SIZE
    y_p, sp = _trunk(x_prompt.reshape(b * t, d), jnp.arange(t, dtype=jnp.int32), None, p, w, b, t, "p")
    past = dict(mla_ckv=cache_mla_ckv, mla_kpe=cache_mla_kpe, sb_k=cache_sb_k, sb_v=cache_sb_v,
                rwkv_state=state_rwkv, rwkv_shift=state_rwkv_shift, gla_state=state_gla, page_table=page_table)
    y_s, ss = _trunk(x_sample.reshape(db, d), past_len + jnp.arange(1, dtype=jnp.int32), past, p, w, db, 1, "s")
    keys = ("mla_ckv", "mla_kpe", "rwkv_state", "rwkv_shift", "sb_k", "sb_v", "gla_state")
    return (y_p, y_s) + tuple(sp[k] for k in keys) + tuple(ss[k] for k in keys)
```

```python
import functools
import math

import jax
import jax.numpy as jnp
from jax import lax
from jax.experimental import pallas as pl
from jax.experimental.pallas import tpu as pltpu

f32 = jnp.float32
bf16 = jnp.bfloat16

PAGE_SIZE = 128
MLA_HEADS = 8
MLA_NOPE = 128
MLA_ROPE = 64
MLA_V = 128
MLA_Q_RANK = 512
MLA_KV_RANK = 512
MLA_SCALE = (MLA_NOPE + MLA_ROPE) ** -0.5
ROPE_THETA = 10000.0
RWKV_HEADS = 16
RWKV_HEAD = 64
RWKV_W = RWKV_HEADS * RWKV_HEAD
RWKV_DECAY_RANK = 64
RWKV_A_RANK = 64
RWKV_GATE_RANK = 160
RWKV_GN_EPS = 64e-5
RWKV_COLS = 3 * RWKV_W + RWKV_DECAY_RANK + RWKV_A_RANK + RWKV_GATE_RANK
SB_HEADS = 16
SB_HEAD = 64
SB_W = SB_HEADS * SB_HEAD
SB_SCALE = SB_HEAD ** -0.5
GLA_HEADS = 4
GLA_DK = 128
GLA_DV = 256
GLA_GATE_RANK = 16
GLA_GATE_NORM = 16.0
NORM_EPS = 1e-6

LANES = 128
MXU_TILE = 256
VMEM_LIMIT_BYTES = 56 * 1024 * 1024

C0_R, C0_K, C0_V = 0, 1024, 2048
C0_QA, C0_KVA = 3072, 3584
C0_KPE, C0_KPEROT = 4096, 4224
C0_WDAD = 4352
C0_GD = 4608
C0_N = 5120
C1_SQ, C1_SK, C1_SV = 0, 1024, 2048
C1_GQ, C1_GK, C1_GV = 3072, 3584, 4096
C1_GOG = 5120
C1_GGD = 6144
C1_N = 6656

RWKV_CHUNK = 64
GLA_CHUNK = 64
MLA_SLOT = 2 * LANES
ABS_SLOT = MLA_KV_RANK + LANES


def _cparams(*sem):
    return pltpu.CompilerParams(dimension_semantics=sem, vmem_limit_bytes=VMEM_LIMIT_BYTES)


def _dot(a, b):
    return jnp.dot(a, b, preferred_element_type=f32)


def _dot_nt(a, b):
    return lax.dot_general(a, b, (((1,), (1,)), ((), ())), preferred_element_type=f32)


def _dot_tn(a, b):
    return lax.dot_general(a, b, (((0,), (0,)), ((), ())), preferred_element_type=f32)


def _split_dot_right(x, m):
    hi = x.astype(bf16)
    lo = (x - hi.astype(f32)).astype(bf16)
    return _dot(hi, m) + _dot(lo, m)


def _split_dot_left(m, x):
    hi = x.astype(bf16)
    lo = (x - hi.astype(f32)).astype(bf16)
    return _dot(m, hi) + _dot(m, lo)


def _sigmoid(x):
    return 1.0 / (1.0 + jnp.exp(-x))


def _softplus(x):
    return jnp.maximum(x, 0.0) + jnp.log(1.0 + jnp.exp(-jnp.abs(x)))


def _rms(x, gain):
    return x * lax.rsqrt(jnp.mean(x * x, axis=-1, keepdims=True) + NORM_EPS) * gain


def _mm_body(*refs, n_lhs, norm, res):
    lhs = refs[:n_lhs]
    ws = refs[n_lhs:2 * n_lhs]
    pos = 2 * n_lhs
    g_ref = refs[pos] if norm else None
    pos += int(norm)
    r_ref = refs[pos] if res else None
    pos += int(res)
    o_ref = refs[pos]
    hs = refs[pos + 1:pos + 1 + n_lhs]

    @pl.when(pl.program_id(1) == 0)
    def _():
        for i in range(n_lhs):
            x = lhs[i][...].astype(f32)
            if norm and i == 0:
                x = _rms(x, g_ref[...])
            hs[i][...] = x.astype(bf16)

    acc = _dot(hs[0][...], ws[0][...])
    for i in range(1, n_lhs):
        acc = acc + _dot(hs[i][...], ws[i][...])
    if res:
        acc = acc + r_ref[...]
    o_ref[...] = acc.astype(o_ref.dtype)


def matmul(lhs_list, w_list, *, gain=None, res=None, out_dtype=f32, tm=512, tn=512, name):
    m = lhs_list[0][0].shape[0]
    n = w_list[0].shape[1]
    tm, tn = min(tm, m), min(tn, n)
    assert m % tm == 0 and n % tn == 0
    in_specs, args, scratch = [], [], []
    for arr, k, cb in lhs_list:
        in_specs.append(pl.BlockSpec((tm, k), lambda i, j, cb=cb: (i, cb)))
        args.append(arr)
        scratch.append(pltpu.VMEM((tm, k), bf16))
    for (arr, k, cb), w in zip(lhs_list, w_list):
        assert w.shape[0] == k
        in_specs.append(pl.BlockSpec((k, tn), lambda i, j: (0, j)))
        args.append(w)
    if gain is not None:
        in_specs.append(pl.BlockSpec((1, lhs_list[0][1]), lambda i, j: (0, 0)))
        args.append(gain.reshape(1, -1))
    if res is not None:
        in_specs.append(pl.BlockSpec((tm, tn), lambda i, j: (i, j)))
        args.append(res)
    body = functools.partial(_mm_body, n_lhs=len(lhs_list), norm=gain is not None, res=res is not None)
    return pl.pallas_call(
        body,
        out_shape=jax.ShapeDtypeStruct((m, n), out_dtype),
        grid=(m // tm, n // tn),
        in_specs=in_specs,
        out_specs=pl.BlockSpec((tm, tn), lambda i, j: (i, j)),
        scratch_shapes=scratch,
        compiler_params=_cparams("parallel", "arbitrary"),
        name=name,
    )(*args)


def _hmm_body(x_ref, w_ref, o_ref):
    o_ref[...] = _dot(x_ref[...].astype(bf16), w_ref[0]).astype(o_ref.dtype)


def head_matmul(x, w, *, name):
    m = x.shape[0]
    h, k, n = w.shape
    return pl.pallas_call(
        _hmm_body,
        out_shape=jax.ShapeDtypeStruct((m, h * n), f32),
        grid=(h,),
        in_specs=[pl.BlockSpec((m, k), lambda i: (0, i)), pl.BlockSpec((1, k, n), lambda i: (i, 0, 0))],
        out_specs=pl.BlockSpec((m, n), lambda i: (0, i)),
        compiler_params=_cparams("arbitrary"),
        name=name,
    )(x, w)


def _ffn_body(*refs, final):
    x_ref, g_ref, wg_ref, wu_ref, wd_ref = refs[:5]
    gf_ref = refs[5] if final else None
    o_ref, h_ref, acc_ref = refs[5 + int(final):]
    f = pl.program_id(1)

    @pl.when(f == 0)
    def _():
        h_ref[...] = _rms(x_ref[...], g_ref[...]).astype(bf16)
        acc_ref[...] = jnp.zeros_like(acc_ref)

    h = h_ref[...]
    a = _dot(h, wg_ref[...])
    u = _dot(h, wu_ref[...])
    s = (a * _sigmoid(a) * u).astype(bf16)
    acc_ref[...] += _dot(s, wd_ref[...])

    @pl.when(f == pl.num_programs(1) - 1)
    def _():
        y = x_ref[...] + acc_ref[...]
        if final:
            y = _rms(y, gf_ref[...])
        o_ref[...] = y


def ffn(x, gain, wg, wu, wd, *, final_gain=None, tm=512, tf=512, name):
    m, d = x.shape
    dff = wg.shape[1]
    tm = min(tm, m)
    assert m % tm == 0 and dff % tf == 0
    in_specs = [
        pl.BlockSpec((tm, d), lambda i, j: (i, 0)),
        pl.BlockSpec((1, d), lambda i, j: (0, 0)),
        pl.BlockSpec((d, tf), lambda i, j: (0, j)),
        pl.BlockSpec((d, tf), lambda i, j: (0, j)),
        pl.BlockSpec((tf, d), lambda i, j: (j, 0)),
    ]
    args = [x, gain.reshape(1, d), wg, wu, wd]
    if final_gain is not None:
        in_specs.append(pl.BlockSpec((1, d), lambda i, j: (0, 0)))
        args.append(final_gain.reshape(1, d))
    return pl.pallas_call(
        functools.partial(_ffn_body, final=final_gain is not None),
        out_shape=jax.ShapeDtypeStruct((m, d), f32),
        grid=(m // tm, dff // tf),
        in_specs=in_specs,
        out_specs=pl.BlockSpec((tm, d), lambda i, j: (i, 0)),
        scratch_shapes=[pltpu.VMEM((tm, d), bf16), pltpu.VMEM((tm, d), f32)],
        compiler_params=_cparams("parallel", "arbitrary"),
        name=name,
    )(*args)


def _mla_prep_body(*refs, absorbed):
    (qa_ref, kva_ref, kpe_ref, kperot_ref, cos_ref, sin_ref, qg_ref, wq_ref, kg_ref, wkv_ref) = refs[:10]
    outs = refs[10:]
    cos, sin = cos_ref[...], sin_ref[...]
    qn = _rms(qa_ref[...], qg_ref[...]).astype(bf16)
    qall = _dot(qn, wq_ref[...])
    hw = MLA_HEADS * LANES
    nope, pe, rot = qall[:, :hw], qall[:, hw:2 * hw], qall[:, 2 * hw:]
    cos8 = jnp.concatenate([cos] * MLA_HEADS, axis=1)
    sin8 = jnp.concatenate([sin] * MLA_HEADS, axis=1)
    roped = pe * cos8 + rot * sin8
    ckv = _rms(kva_ref[...], kg_ref[...])
    kpe = kpe_ref[...] * cos + kperot_ref[...] * sin
    if absorbed:
        qcat_ref, ckv_ref, kpe_out_ref, knew_ref = outs
        pieces = []
        for h in range(MLA_HEADS):
            qabs = _dot(nope[:, h * LANES:(h + 1) * LANES].astype(bf16), wkv_ref[h])
            pieces += [qabs, roped[:, h * LANES:(h + 1) * LANES]]
        qcat_ref[...] = jnp.concatenate(pieces, axis=1).astype(bf16)
        knew_ref[...] = jnp.concatenate([ckv, kpe], axis=1)
    else:
        qcat_ref, kcat_ref, v_ref, ckv_ref, kpe_out_ref = outs
        kv = _dot(ckv.astype(bf16), wkv_ref[...])
        qp, kp = [], []
        for h in range(MLA_HEADS):
            sl = slice(h * LANES, (h + 1) * LANES)
            qp += [nope[:, sl], roped[:, sl]]
            kp += [kv[:, sl], kpe]
        qcat_ref[...] = jnp.concatenate(qp, axis=1).astype(bf16)
        kcat_ref[...] = jnp.concatenate(kp, axis=1).astype(bf16)
        v_ref[...] = kv[:, hw:].astype(bf16)
    ckv_ref[...] = ckv
    kpe_out_ref[...] = kpe[:, :MLA_ROPE]


def mla_prep(cols0, cos, sin, q_gain, wq_all, kv_gain, wkv, *, absorbed, tm=256, name):
    m = cols0.shape[0]
    tm = min(tm, m)
    assert m % tm == 0
    row = lambda w, cb: pl.BlockSpec((tm, w), lambda i, cb=cb: (i, cb))
    full = lambda a: pl.BlockSpec(a.shape, lambda i, nd=a.ndim: (0,) * nd)
    qg, kg = q_gain.reshape(1, -1), kv_gain.reshape(1, -1)
    in_specs = [
        row(MLA_Q_RANK, C0_QA // MLA_Q_RANK), row(MLA_KV_RANK, C0_KVA // MLA_KV_RANK),
        row(LANES, C0_KPE // LANES), row(LANES, C0_KPEROT // LANES),
        row(LANES, 0), row(LANES, 0), full(qg), full(wq_all), full(kg), full(wkv),
    ]
    if absorbed:
        out_shape = [jax.ShapeDtypeStruct((m, MLA_HEADS * ABS_SLOT), bf16),
                     jax.ShapeDtypeStruct((m, MLA_KV_RANK), f32),
                     jax.ShapeDtypeStruct((m, MLA_ROPE), f32),
                     jax.ShapeDtypeStruct((m, ABS_SLOT), f32)]
        out_specs = [row(MLA_HEADS * ABS_SLOT, 0), row(MLA_KV_RANK, 0), row(MLA_ROPE, 0), row(ABS_SLOT, 0)]
    else:
        out_shape = [jax.ShapeDtypeStruct((m, MLA_HEADS * MLA_SLOT), bf16),
                     jax.ShapeDtypeStruct((m, MLA_HEADS * MLA_SLOT), bf16),
                     jax.ShapeDtypeStruct((m, MLA_HEADS * MLA_V), bf16),
                     jax.ShapeDtypeStruct((m, MLA_KV_RANK), f32),
                     jax.ShapeDtypeStruct((m, MLA_ROPE), f32)]
        out_specs = [row(MLA_HEADS * MLA_SLOT, 0), row(MLA_HEADS * MLA_SLOT, 0), row(MLA_HEADS * MLA_V, 0),
                     row(MLA_KV_RANK, 0), row(MLA_ROPE, 0)]
    return pl.pallas_call(
        functools.partial(_mla_prep_body, absorbed=absorbed),
        out_shape=out_shape,
        grid=(m // tm,),
        in_specs=in_specs,
        out_specs=out_specs,
        compiler_params=_cparams("parallel"),
        name=name,
    )(cols0, cols0, cols0, cols0, cos, sin, qg, wq_all, kg, wkv)


def _causal_pairs(nblk, reverse):
    qi, kj = [], []
    for i in range(nblk):
        ks = range(i, -1, -1) if reverse else range(i + 1)
        for j in ks:
            qi.append(i)
            kj.append(j)
    return jnp.asarray(qi, jnp.int32), jnp.asarray(kj, jnp.int32)


def _mla_flash_body(qi_ref, kj_ref, q_ref, k_ref, v_ref, o_ref, m_ref, l_ref, acc_ref, *, hp):
    p = pl.program_id(2)
    i, j = qi_ref[p], kj_ref[p]
    heads = range(hp)

    @pl.when(j == 0)
    def _():
        m_ref[...] = jnp.full_like(m_ref, -jnp.inf)
        l_ref[...] = jnp.zeros_like(l_ref)
        acc_ref[...] = jnp.zeros_like(acc_ref)

    def step(diagonal):
        q, k, v = q_ref[...], k_ref[...], v_ref[...]
        qk = [slice(h * MLA_SLOT, (h + 1) * MLA_SLOT) for h in heads]
        s = [_dot_nt(q[:, qk[h]], k[:, qk[h]]) * MLA_SCALE for h in heads]
        if diagonal:
            keep = (lax.broadcasted_iota(jnp.int32, s[0].shape, 1) <= lax.broadcasted_iota(jnp.int32, s[0].shape, 0))
            s = [jnp.where(keep, x, -jnp.inf) for x in s]
        m_prev = [m_ref[h] for h in heads]
        m_new = [jnp.maximum(m_prev[h], jnp.max(s[h], axis=-1, keepdims=True)) for h in heads]
        alpha = [jnp.exp(m_prev[h] - m_new[h]) for h in heads]
        pr = [jnp.exp(s[h] - m_new[h]) for h in heads]
        for h in heads:
            l_ref[h] = l_ref[h] * alpha[h] + jnp.sum(pr[h], axis=-1, keepdims=True)
            acc_ref[h] = acc_ref[h] * alpha[h] + _dot(pr[h].astype(bf16), v[:, h * MLA_V:(h + 1) * MLA_V])
            m_ref[h] = m_new[h]

    @pl.when(j < i)
    def _():
        step(False)

    @pl.when(j == i)
    def _():
        step(True)
        o_ref[...] = jnp.concatenate([acc_ref[h] / l_ref[h] for h in heads], axis=1)


def mla_flash(qcat, kcat, v, batch, seq, *, tq=512, hp=4, name):
    tq = min(tq, seq)
    nblk = seq // tq
    qi, kj = _causal_pairs(nblk, reverse=False)
    grid_spec = pltpu.PrefetchScalarGridSpec(
        num_scalar_prefetch=2,
        grid=(batch, MLA_HEADS // hp, int(qi.shape[0])),
        in_specs=[
            pl.BlockSpec((tq, hp * MLA_SLOT), lambda b, h, p, qi, kj: (b * nblk + qi[p], h)),
            pl.BlockSpec((tq, hp * MLA_SLOT), lambda b, h, p, qi, kj: (b * nblk + kj[p], h)),
            pl.BlockSpec((tq, hp * MLA_V), lambda b, h, p, qi, kj: (b * nblk + kj[p], h)),
        ],
        out_specs=pl.BlockSpec((tq, hp * MLA_V), lambda b, h, p, qi, kj: (b * nblk + qi[p], h)),
        scratch_shapes=[pltpu.VMEM((hp, tq, 1), f32), pltpu.VMEM((hp, tq, 1), f32),
                        pltpu.VMEM((hp, tq, MLA_V), f32)],
    )
    return pl.pallas_call(
        functools.partial(_mla_flash_body, hp=hp),
        out_shape=jax.ShapeDtypeStruct((batch * seq, MLA_HEADS * MLA_V), f32),
        grid_spec=grid_spec,
        compiler_params=_cparams("parallel", "parallel", "arbitrary"),
        name=name,
    )(qi, kj, qcat, kcat, v)


def _sb_flash_body(qi_ref, kj_ref, q_ref, k_ref, v_ref, u_ref, o_ref, acc_ref, run_ref, *, tq, hp):
    p = pl.program_id(2)
    i, j = qi_ref[p], kj_ref[p]

    heads = range(hp)

    def step(diagonal):
        q = (q_ref[...] * SB_SCALE).astype(bf16)
        k, v = k_ref[...].astype(bf16), v_ref[...].astype(bf16)
        upper = u_ref[...]
        sls = [slice(h * SB_HEAD, (h + 1) * SB_HEAD) for h in heads]
        z = [_dot_nt(q[:, sl], k[:, sl]) for sl in sls]
        log_fail = [-_softplus(x) for x in z]
        if diagonal:
            valid = lax.broadcasted_iota(jnp.int32, z[0].shape, 1) < lax.broadcasted_iota(jnp.int32, z[0].shape, 0)
            log_fail = [jnp.where(valid, x, 0.0) for x in log_fail]
        sw = upper.shape[0]
        nsub = z[0].shape[1] // sw
        after, total = [], []
        for h in heads:
            parts = [log_fail[h][:, c * sw:(c + 1) * sw] for c in range(nsub)]
            sums = [jnp.sum(x, axis=-1, keepdims=True) for x in parts]
            local = [_split_dot_right(x, upper) for x in parts]
            later = sums[nsub - 1]
            for c in range(nsub - 2, -1, -1):
                local[c] = local[c] + later
                later = later + sums[c]
            after.append(jnp.concatenate(local, axis=1) if nsub > 1 else local[0])
            total.append(later)
        if diagonal:
            w = [jnp.where(valid, jnp.exp(z[h] + log_fail[h] + after[h]), 0.0) for h in heads]
        else:
            w = [jnp.exp(z[h] + log_fail[h] + after[h] + run_ref[h]) for h in heads]
        for h in heads:
            part = _dot(w[h].astype(bf16), v[:, sls[h]])
            if diagonal:
                acc_ref[h] = part
                run_ref[h] = total[h]
            else:
                acc_ref[h] += part
                run_ref[h] += total[h]

    @pl.when(j == i)
    def _():
        step(True)

    @pl.when(j < i)
    def _():
        step(False)

    @pl.when(j == 0)
    def _():
        o_ref[...] = jnp.concatenate([acc_ref[h] for h in heads], axis=1)


def sb_flash(cols1, batch, seq, *, tq=512, hp=4, name):
    tq = min(tq, seq)
    nblk = seq // tq
    qi, kj = _causal_pairs(nblk, reverse=True)
    w = hp * SB_HEAD
    nq, nk, nv = C1_SQ // w, C1_SK // w, C1_SV // w
    sw = min(tq, MXU_TILE)
    assert tq % sw == 0
    upper = (lax.broadcasted_iota(jnp.int32, (sw, sw), 0) > lax.broadcasted_iota(jnp.int32, (sw, sw), 1)).astype(bf16)
    grid_spec = pltpu.PrefetchScalarGridSpec(
        num_scalar_prefetch=2,
        grid=(batch, SB_HEADS // hp, int(qi.shape[0])),
        in_specs=[
            pl.BlockSpec((tq, w), lambda b, h, p, qi, kj: (b * nblk + qi[p], nq + h)),
            pl.BlockSpec((tq, w), lambda b, h, p, qi, kj: (b * nblk + kj[p], nk + h)),
            pl.BlockSpec((tq, w), lambda b, h, p, qi, kj: (b * nblk + kj[p], nv + h)),
            pl.BlockSpec((sw, sw), lambda b, h, p, qi, kj: (0, 0)),
        ],
        out_specs=pl.BlockSpec((tq, w), lambda b, h, p, qi, kj: (b * nblk + qi[p], h)),
        scratch_shapes=[pltpu.VMEM((hp, tq, SB_HEAD), f32), pltpu.VMEM((hp, tq, 1), f32)],
    )
    return pl.pallas_call(
        functools.partial(_sb_flash_body, tq=tq, hp=hp),
        out_shape=jax.ShapeDtypeStruct((batch * seq, SB_W), f32),
        grid_spec=grid_spec,
        compiler_params=_cparams("parallel", "parallel", "arbitrary"),
        name=name,
    )(qi, kj, cols1, cols1, cols1, upper)


def _rwkv_gates_body(*refs, seq_rows):
    (r_ref, k_ref, v_ref, wdad_ref, gd_ref) = refs[:5]
    pos = 5
    if seq_rows == 1:
        prevs = [ref[...] for ref in refs[pos:pos + 5]]
        pos += 5
    else:
        tails = refs[pos:pos + 5]
        firsts = refs[pos + 5:pos + 10]
        pos += 10
    (mu_r, mu_k, mu_v, mu_wdad, mu_gd, w0_ref, w2_ref, a0_ref, a2_ref, g2_ref, kk_ref, ka_ref, ones_ref) = refs[pos:pos + 13]
    (r_out, k_out, v_out, kkn_out, a_out, lw_out, g_out) = refs[pos + 13:]
    cur = [r_ref[...], k_ref[...], v_ref[...], wdad_ref[...], gd_ref[...]]
    if seq_rows != 1:
        tm = cur[0].shape[0]
        starts_seq = (pl.program_id(0) * tm) % seq_rows == 0
        prevs = []
        for x, tail, first in zip(cur, tails, firsts):
            carry = jnp.where(starts_seq, first[0], tail[7:8, :])
            rolled = pltpu.roll(x, 1, 0)
            rowid = lax.broadcasted_iota(jnp.int32, x.shape, 0)
            prevs.append(jnp.where(rowid == 0, carry, rolled))
    mus = [mu_r[...], mu_k[...], mu_v[...], mu_wdad[...], mu_gd[...]]
    xr, xk, xv, xwdad, xgd = [c + (p - c) * m for c, p, m in zip(cur, prevs, mus)]
    w_log = -_softplus(-(w0_ref[...] + _dot(jnp.tanh(xwdad).astype(bf16), w2_ref[...]))) - 0.5
    lw_out[...] = -jnp.exp(w_log)
    a = _sigmoid(a0_ref[...] + _dot(xwdad.astype(bf16), a2_ref[...]))
    g_out[...] = _dot(_sigmoid(xgd).astype(bf16), g2_ref[...])
    kk = xk * kk_ref[...]
    ssq = _split_dot_right(kk * kk, ones_ref[...])
    kkn_out[...] = kk / jnp.maximum(jnp.sqrt(ssq), 1e-12)
    k_out[...] = xk * (1.0 + (a - 1.0) * ka_ref[...])
    r_out[...] = xr
    v_out[...] = xv
    a_out[...] = a


def rwkv_gates(cols0, shift_p, seq_rows, prm, *, tm=256, name):
    m = cols0.shape[0]
    tm = min(tm, m, seq_rows) if seq_rows != 1 else min(tm, m)
    assert m % tm == 0 and (seq_rows == 1 or (seq_rows % tm == 0 and tm % 8 == 0))
    groups = [(RWKV_W, C0_R // RWKV_W), (RWKV_W, C0_K // RWKV_W), (RWKV_W, C0_V // RWKV_W),
              (LANES, C0_WDAD // LANES), (2 * LANES, C0_GD // (2 * LANES))]
    in_specs = [pl.BlockSpec((tm, w), lambda i, cb=cb: (i, cb)) for w, cb in groups]
    args = [cols0] * 5
    if seq_rows == 1:
        in_specs += [pl.BlockSpec((tm, w), lambda i, cb=cb: (i, cb)) for w, cb in groups]
        args += [shift_p] * 5
    else:
        per = tm // 8
        in_specs += [pl.BlockSpec((8, w), lambda i, cb=cb: (jnp.maximum(i * per - 1, 0), cb)) for w, cb in groups]
        args += [cols0] * 5
        shift3 = shift_p.reshape(shift_p.shape[0], 1, C0_N)
        in_specs += [pl.BlockSpec((1, 1, w), lambda i, cb=cb: ((i * tm) // seq_rows, 0, cb)) for w, cb in groups]
        args += [shift3] * 5
    small = [prm["mu_r"], prm["mu_k"], prm["mu_v"], prm["mu_wdad"], prm["mu_gd"], prm["w0"], prm["w2p"],
             prm["a0"], prm["a2p"], prm["g2p"], prm["k_k"], prm["k_a"], prm["head_ones"]]
    in_specs += [pl.BlockSpec(a.shape, lambda i: (0, 0)) for a in small]
    args += small
    out_spec = pl.BlockSpec((tm, RWKV_W), lambda i: (i, 0))
    return pl.pallas_call(
        functools.partial(_rwkv_gates_body, seq_rows=seq_rows),
        out_shape=[jax.ShapeDtypeStruct((m, RWKV_W), f32)] * 7,
        grid=(m // tm,),
        in_specs=in_specs,
        out_specs=[out_spec] * 7,
        compiler_params=_cparams("parallel"),
        name=name,
    )(*args)


def _rwkv_scan_body(r_ref, k_ref, v_ref, kk_ref, a_ref, lw_ref, g_ref, s0_ref, rk_ref, lnw_ref, lnb_ref,
                    y_ref, s_out_ref, s_ref):
    c = pl.program_id(1)
    C = r_ref.shape[0]

    @pl.when(c == 0)
    def _():
        s_ref[...] = s0_ref[0]

    rowi = lax.broadcasted_iota(jnp.int32, (C, C), 0)
    coli = lax.broadcasted_iota(jnp.int32, (C, C), 1)
    incl = coli <= rowi
    strict = coli < rowi
    lower = incl.astype(bf16)
    r, k, v, kk, a, lw, g = (x[...] for x in (r_ref, k_ref, v_ref, kk_ref, a_ref, lw_ref, g_ref))
    cum = _split_dot_left(lower, lw)
    gam = jnp.exp(cum)
    ginv = jnp.exp(-cum)
    a_m = -kk * jnp.exp(cum - lw)
    b_m = kk * a * ginv
    k_m = k * ginv
    r_m = r * gam
    cum_last = cum[C - 1:C, :]
    g_last = jnp.exp(cum_last)
    b_end = kk * a * jnp.exp(cum_last - cum)
    k_end = k * jnp.exp(cum_last - cum)
    bonus_w = r * k * rk_ref[...]
    lnw, lnb = lnw_ref[...], lnb_ref[...]
    heads = range(RWKV_HEADS)
    sls = [slice(h * RWKV_HEAD, (h + 1) * RWKV_HEAD) for h in heads]
    ar = [jnp.concatenate([a_m[:, sl], r_m[:, sl]], axis=0).astype(bf16) for sl in sls]
    vh = [v[:, sl].astype(bf16) for sl in sls]
    s_old = [s_ref[h] for h in heads]
    g_b = [_dot_nt(ar[h], b_m[:, sls[h]].astype(bf16)) for h in heads]
    g_k = [_dot_nt(ar[h], k_m[:, sls[h]].astype(bf16)) for h in heads]
    g_s = [_dot_nt(ar[h], s_old[h].astype(bf16)) for h in heads]
    x = [g_s[h][:C] + _dot(jnp.where(strict, g_k[h][:C], 0.0).astype(bf16), vh[h]) for h in heads]
    n = [jnp.where(strict, g_b[h][:C], 0.0).astype(bf16) for h in heads]
    steps = int(math.log2(C))
    for it in range(steps):
        x = [x[h] + _dot(n[h], x[h].astype(bf16)) for h in heads]
        if it + 1 < steps:
            n = [_dot(n[h], n[h]).astype(bf16) for h in heads]
    xb = [x[h].astype(bf16) for h in heads]
    y = [g_s[h][C:] + _dot(jnp.where(incl, g_b[h][C:], 0.0).astype(bf16), xb[h])
         + _dot(jnp.where(incl, g_k[h][C:], 0.0).astype(bf16), vh[h]) for h in heads]
    for h in heads:
        pv = jnp.concatenate([xb[h], vh[h]], axis=0)
        bk_end = jnp.concatenate([b_end[:, sls[h]], k_end[:, sls[h]]], axis=0).astype(bf16)
        s_ref[h] = s_old[h] * g_last[:, sls[h]] + _dot_tn(pv, bk_end)
    pieces = []
    for h in heads:
        sl = sls[h]
        mean = jnp.mean(y[h], axis=-1, keepdims=True)
        var = jnp.mean(jnp.square(y[h] - mean), axis=-1, keepdims=True)
        yn = (y[h] - mean) * lax.rsqrt(var + RWKV_GN_EPS) * lnw[:, sl] + lnb[:, sl]
        bonus = jnp.sum(bonus_w[:, sl], axis=-1, keepdims=True) * v[:, sl]
        pieces.append((yn + bonus) * g[:, sl])
    y_ref[...] = jnp.concatenate(pieces, axis=1)

    @pl.when(c == pl.num_programs(1) - 1)
    def _():
        s_out_ref[0] = s_ref[...]


def rwkv_scan(gates, s0, prm, batch, seq, *, name):
    C = min(RWKV_CHUNK, seq)
    nch = seq // C
    tok = pl.BlockSpec((C, RWKV_W), lambda b, c: (b * nch + c, 0))
    st = pl.BlockSpec((1, RWKV_HEADS, RWKV_HEAD, RWKV_HEAD), lambda b, c: (b, 0, 0, 0))
    vec = pl.BlockSpec((1, RWKV_W), lambda b, c: (0, 0))
    r, k, v, kkn, a, lw, g = gates
    return pl.pallas_call(
        _rwkv_scan_body,
        out_shape=[jax.ShapeDtypeStruct((batch * seq, RWKV_W), f32),
                   jax.ShapeDtypeStruct((batch, RWKV_HEADS, RWKV_HEAD, RWKV_HEAD), f32)],
        grid=(batch, nch),
        in_specs=[tok] * 7 + [st, vec, vec, vec],
        out_specs=[tok, st],
        scratch_shapes=[pltpu.VMEM((RWKV_HEADS, RWKV_HEAD, RWKV_HEAD), f32)],
        compiler_params=_cparams("parallel", "arbitrary"),
        name=name,
    )(r, k, v, kkn, a, lw, g, s0, prm["r_k"], prm["ln_w"], prm["ln_b"])


def _rwkv_step_body(r_ref, k_ref, v_ref, kk_ref, a_ref, lw_ref, g_ref, s0_ref, rk_ref, lnw_ref, lnb_ref,
                    y_ref, s_out_ref):
    n = RWKV_HEAD
    eye = (lax.broadcasted_iota(jnp.int32, (n, n), 0) == lax.broadcasted_iota(jnp.int32, (n, n), 1)).astype(f32)
    r, k, v, kk, a, lw, g = (x[...] for x in (r_ref, k_ref, v_ref, kk_ref, a_ref, lw_ref, g_ref))
    s = s0_ref[...]
    sa = jnp.sum(s * (-kk), axis=-1, keepdims=True)
    vcol = jnp.sum(eye * v, axis=-1, keepdims=True)
    s = s * jnp.exp(lw) + sa * (kk * a) + vcol * k
    s_out_ref[...] = s
    ycol = jnp.sum(s * r, axis=-1, keepdims=True)
    y = jnp.sum(eye * ycol, axis=-2, keepdims=True)
    mean = jnp.mean(y, axis=-1, keepdims=True)
    var = jnp.mean(jnp.square(y - mean), axis=-1, keepdims=True)
    y = (y - mean) * lax.rsqrt(var + RWKV_GN_EPS) * lnw_ref[...] + lnb_ref[...]
    bonus = jnp.sum(r * k * rk_ref[...], axis=-1, keepdims=True) * v
    y_ref[...] = (y + bonus) * g


def rwkv_step(gates, s0, prm, *, bs=8, name):
    m = s0.shape[0]
    bs = min(bs, m)
    assert m % bs == 0
    h4 = lambda x: x.reshape(-1, RWKV_HEADS, 1, RWKV_HEAD)
    vec = pl.BlockSpec((bs, RWKV_HEADS, 1, RWKV_HEAD), lambda i: (i, 0, 0, 0))
    st = pl.BlockSpec((bs, RWKV_HEADS, RWKV_HEAD, RWKV_HEAD), lambda i: (i, 0, 0, 0))
    par = pl.BlockSpec((1, RWKV_HEADS, 1, RWKV_HEAD), lambda i: (0, 0, 0, 0))
    y, s = pl.pallas_call(
        _rwkv_step_body,
        out_shape=[jax.ShapeDtypeStruct((m, RWKV_HEADS, 1, RWKV_HEAD), f32), jax.ShapeDtypeStruct(s0.shape, f32)],
        grid=(m // bs,),
        in_specs=[vec] * 7 + [st, par, par, par],
        out_specs=[vec, st],
        compiler_params=_cparams("parallel"),
        name=name,
    )(*[h4(x) for x in gates], s0, h4(prm["r_k"]), h4(prm["ln_w"]), h4(prm["ln_b"]))
    return y.reshape(m, RWKV_W), s


def _gla_log_gate(ggd, wgup, bg):
    x = _dot(ggd.astype(bf16), wgup) + bg
    return -_softplus(-x) / GLA_GATE_NORM


def _gla_out(o, gain, gog):
    return _rms(o, gain) * (gog * _sigmoid(gog))


def _gla_scan_body(q_ref, k_ref, v_ref, ggd_ref, gog_ref, s0_ref, wgup_ref, bg_ref, gn_ref,
                   d_ref, s_out_ref, s_ref):
    c = pl.program_id(1)
    C = q_ref.shape[0]

    @pl.when(c == 0)
    def _():
        s_ref[...] = s0_ref[0]

    rowi = lax.broadcasted_iota(jnp.int32, (C, C), 0)
    coli = lax.broadcasted_iota(jnp.int32, (C, C), 1)
    incl = coli <= rowi
    lower = incl.astype(bf16)
    eye = (lax.broadcasted_iota(jnp.int32, (GLA_DK, GLA_DK), 0)
           == lax.broadcasted_iota(jnp.int32, (GLA_DK, GLA_DK), 1)).astype(f32)
    la = _gla_log_gate(ggd_ref[...], wgup_ref[...], bg_ref[...])
    b = _split_dot_left(lower, la)
    mid = (C // 2) // 8 * 8
    b_mid = b[mid:mid + 1, :]
    b_last = b[C - 1:C, :]
    qs = q_ref[...] * GLA_DK ** -0.5
    k, v, gog = k_ref[...], v_ref[...], gog_ref[...]
    q_inter = (qs * jnp.exp(b)).astype(bf16)
    q_mid = (qs * jnp.exp(b - b_mid)).astype(bf16)
    k_mid = (k * jnp.exp(b_mid - b)).astype(bf16)
    k_end = (k * jnp.exp(b_last - b)).astype(bf16)
    e_last = jnp.exp(b_last)
    heads = range(GLA_HEADS)
    ks = [slice(h * GLA_DK, (h + 1) * GLA_DK) for h in heads]
    vs = [slice(h * GLA_DV, (h + 1) * GLA_DV) for h in heads]
    s_old = [s_ref[h] for h in heads]
    vh = [v[:, vs[h]].astype(bf16) for h in heads]
    att = [jnp.where(incl, _dot_nt(q_mid[:, ks[h]], k_mid[:, ks[h]]), 0.0).astype(bf16) for h in heads]
    o = [_dot(q_inter[:, ks[h]], s_old[h].astype(bf16)) + _dot(att[h], vh[h]) for h in heads]
    e_col = [jnp.sum(eye * e_last[:, ks[h]], axis=-1, keepdims=True) for h in heads]
    for h in heads:
        s_ref[h] = s_old[h] * e_col[h] + _dot_tn(k_end[:, ks[h]], vh[h])
    d_ref[...] = jnp.concatenate([_gla_out(o[h], gn_ref[...], gog[:, vs[h]]) for h in heads], axis=1)

    @pl.when(c == pl.num_programs(1) - 1)
    def _():
        s_out_ref[0] = s_ref[...]


def gla_scan(cols1, s0, prm, batch, seq, *, name):
    C = min(GLA_CHUNK, seq)
    nch = seq // C
    kw, vw = GLA_HEADS * GLA_DK, GLA_HEADS * GLA_DV
    tok = lambda w, off: pl.BlockSpec((C, w), lambda b, c, cb=off // w: (b * nch + c, cb))
    st = pl.BlockSpec((1, GLA_HEADS, GLA_DK, GLA_DV), lambda b, c: (b, 0, 0, 0))
    full = lambda a: pl.BlockSpec(a.shape, lambda b, c: (0, 0))
    return pl.pallas_call(
        _gla_scan_body,
        out_shape=[jax.ShapeDtypeStruct((batch * seq, vw), f32),
                   jax.ShapeDtypeStruct((batch, GLA_HEADS, GLA_DK, GLA_DV), f32)],
        grid=(batch, nch),
        in_specs=[tok(kw, C1_GQ), tok(kw, C1_GK), tok(vw, C1_GV), tok(LANES, C1_GGD), tok(vw, C1_GOG), st,
                  full(prm["wgup"]), full(prm["bg"]), full(prm["gn"])],
        out_specs=[pl.BlockSpec((C, vw), lambda b, c: (b * nch + c, 0)), st],
        scratch_shapes=[pltpu.VMEM((GLA_HEADS, GLA_DK, GLA_DV), f32)],
        compiler_params=_cparams("parallel", "arbitrary"),
        name=name,
    )(cols1, cols1, cols1, cols1, cols1, s0, prm["wgup"], prm["bg"], prm["gn"])


def _gla_gate_body(ggd_ref, wgup_ref, bg_ref, la_ref):
    la_ref[...] = _gla_log_gate(ggd_ref[...], wgup_ref[...], bg_ref[...])


def gla_gate(cols1, prm, *, name):
    m = cols1.shape[0]
    kw = GLA_HEADS * GLA_DK
    return pl.pallas_call(
        _gla_gate_body,
        out_shape=jax.ShapeDtypeStruct((m, kw), f32),
        grid=(1,),
        in_specs=[pl.BlockSpec((m, LANES), lambda i: (0, C1_GGD // LANES)),
                  pl.BlockSpec(prm["wgup"].shape, lambda i: (0, 0)), pl.BlockSpec(prm["bg"].shape, lambda i: (0, 0))],
        out_specs=pl.BlockSpec((m, kw), lambda i: (0, 0)),
        compiler_params=_cparams("arbitrary"),
        name=name,
    )(cols1, prm["wgup"], prm["bg"])


def _gla_step_body(q_ref, k_ref, la_ref, v_ref, gog_ref, s0_ref, gn_ref, d_ref, s_out_ref):
    n = GLA_DK
    eye = (lax.broadcasted_iota(jnp.int32, (n, n), 0) == lax.broadcasted_iota(jnp.int32, (n, n), 1)).astype(f32)
    col = lambda x: jnp.sum(eye * x, axis=-1, keepdims=True)
    qc = col(q_ref[...] * GLA_DK ** -0.5)
    kc = col(k_ref[...])
    ec = col(jnp.exp(la_ref[...]))
    s = s0_ref[...] * ec + kc * v_ref[...]
    s_out_ref[...] = s
    o = jnp.sum(qc * s, axis=-2, keepdims=True)
    d_ref[...] = _gla_out(o, gn_ref[...], gog_ref[...])


def gla_step(q, k, la, v, gog, s0, prm, *, bs=8, name):
    m = s0.shape[0]
    bs = min(bs, m)
    assert m % bs == 0
    hk = lambda x: x.reshape(m, GLA_HEADS, 1, GLA_DK)
    hv = lambda x: x.reshape(m, GLA_HEADS, 1, GLA_DV)
    ks = pl.BlockSpec((bs, GLA_HEADS, 1, GLA_DK), lambda i: (i, 0, 0, 0))
    vs = pl.BlockSpec((bs, GLA_HEADS, 1, GLA_DV), lambda i: (i, 0, 0, 0))
    st = pl.BlockSpec((bs, GLA_HEADS, GLA_DK, GLA_DV), lambda i: (i, 0, 0, 0))
    gn = prm["gn"].reshape(1, 1, 1, GLA_DV)
    d, s = pl.pallas_call(
        _gla_step_body,
        out_shape=[jax.ShapeDtypeStruct((m, GLA_HEADS, 1, GLA_DV), f32), jax.ShapeDtypeStruct(s0.shape, f32)],
        grid=(m // bs,),
        in_specs=[ks, ks, ks, vs, vs, st, pl.BlockSpec(gn.shape, lambda i: (0, 0, 0, 0))],
        out_specs=[vs, st],
        compiler_params=_cparams("parallel"),
        name=name,
    )(hk(q), hk(k), hk(la), hv(v), hv(gog), s0, gn)
    return d.reshape(m, GLA_HEADS * GLA_DV), s


def _kv_token_minor_body(k_ref, v_ref, ko_ref, vo_ref):
    rows = k_ref.shape[0]
    ko_ref[0] = k_ref[...].T.reshape(SB_HEADS, SB_HEAD, rows)
    vo_ref[0] = v_ref[...].T.reshape(SB_HEADS, SB_HEAD, rows)


def kv_token_minor(cols1, groups, rows, *, tr=512, name):
    tr = min(tr, rows)
    assert rows % tr == 0
    nblk = rows // tr
    src = lambda off: pl.BlockSpec((tr, SB_W), lambda g, t, cb=off // SB_W: (g * nblk + t, cb))
    dst = pl.BlockSpec((1, SB_HEADS, SB_HEAD, tr), lambda g, t: (g, 0, 0, t))
    shape = jax.ShapeDtypeStruct((groups, SB_HEADS, SB_HEAD, rows), f32)
    return pl.pallas_call(
        _kv_token_minor_body,
        out_shape=[shape, shape],
        grid=(groups, nblk),
        in_specs=[src(C1_SK), src(C1_SV)],
        out_specs=[dst, dst],
        compiler_params=_cparams("parallel", "parallel"),
        name=name,
    )(cols1, cols1)


def _mla_decode_body(*refs, pages):
    pt_ref, q_ref, knew_ref = refs[:3]
    ckv_refs = refs[3:3 + pages]
    kpe_refs = refs[3 + pages:3 + 2 * pages]
    o_ref, m_ref, l_ref, acc_ref = refs[3 + 2 * pages:]
    j = pl.program_id(1)
    q = q_ref[0]
    q_abs, q_pe = q[:, :MLA_KV_RANK], q[:, MLA_KV_RANK:MLA_KV_RANK + MLA_ROPE]

    groups = m_ref.shape[0]
    per = pages // groups

    @pl.when(j == 0)
    def _():
        kn = knew_ref[0].astype(bf16).astype(f32)
        s_new = jnp.sum(q.astype(f32) * kn, axis=-1, keepdims=True) * MLA_SCALE
        m_ref[...] = jnp.full_like(m_ref, -jnp.inf)
        l_ref[...] = jnp.zeros_like(l_ref)
        acc_ref[...] = jnp.zeros_like(acc_ref)
        m_ref[0] = s_new
        l_ref[0] = jnp.ones_like(s_new)
        acc_ref[0] = jnp.broadcast_to(kn[:, :MLA_KV_RANK], acc_ref.shape[1:])

    cks = [r[0].astype(bf16) for r in ckv_refs]
    scores = [(_dot_nt(q_abs, ck) + _dot(q_pe, kr[0].astype(bf16))) * MLA_SCALE
              for ck, kr in zip(cks, kpe_refs)]
    gs = range(groups)
    s = [jnp.concatenate(scores[g * per:(g + 1) * per], axis=1) for g in gs]
    m_prev = [m_ref[g] for g in gs]
    m_new = [jnp.maximum(m_prev[g], jnp.max(s[g], axis=-1, keepdims=True)) for g in gs]
    alpha = [jnp.exp(m_prev[g] - m_new[g]) for g in gs]
    pr = [jnp.exp(s[g] - m_new[g]) for g in gs]
    for g in gs:
        l_ref[g] = l_ref[g] * alpha[g] + jnp.sum(pr[g], axis=-1, keepdims=True)
        acc = acc_ref[g] * alpha[g]
        for i in range(per):
            acc = acc + _dot(pr[g][:, i * PAGE_SIZE:(i + 1) * PAGE_SIZE].astype(bf16), cks[g * per + i])
        acc_ref[g] = acc
        m_ref[g] = m_new[g]

    @pl.when(j == pl.num_programs(1) - 1)
    def _():
        m_all = m_ref[0]
        for g in range(1, groups):
            m_all = jnp.maximum(m_all, m_ref[g])
        scale = [jnp.exp(m_ref[g] - m_all) for g in gs]
        num = sum(acc_ref[g] * scale[g] for g in gs)
        den = sum(l_ref[g] * scale[g] for g in gs)
        o_ref[0] = num / den


def mla_decode(qcat, knew, cache_ckv, cache_kpe, page_table, *, pages=32, groups=2, name):
    nseq, npages = page_table.shape
    pages = min(pages, npages)
    groups = min(groups, pages)
    assert npages % pages == 0 and pages % groups == 0
    q3 = qcat.reshape(nseq, MLA_HEADS, ABS_SLOT)
    kn3 = knew.reshape(nseq, 1, ABS_SLOT)
    kpe_t = jnp.transpose(cache_kpe, (0, 2, 1))
    page = lambda r, c, i: pl.BlockSpec((1, r, c), lambda b, j, pt, i=i: (pt[b, j * pages + i], 0, 0))
    grid_spec = pltpu.PrefetchScalarGridSpec(
        num_scalar_prefetch=1,
        grid=(nseq, npages // pages),
        in_specs=[pl.BlockSpec((1, MLA_HEADS, ABS_SLOT), lambda b, j, pt: (b, 0, 0)),
                  pl.BlockSpec((1, 1, ABS_SLOT), lambda b, j, pt: (b, 0, 0))]
        + [page(PAGE_SIZE, MLA_KV_RANK, i) for i in range(pages)]
        + [page(MLA_ROPE, PAGE_SIZE, i) for i in range(pages)],
        out_specs=pl.BlockSpec((1, MLA_HEADS, MLA_KV_RANK), lambda b, j, pt: (b, 0, 0)),
        scratch_shapes=[pltpu.VMEM((groups, MLA_HEADS, 1), f32), pltpu.VMEM((groups, MLA_HEADS, 1), f32),
                        pltpu.VMEM((groups, MLA_HEADS, MLA_KV_RANK), f32)],
    )
    lat = pl.pallas_call(
        functools.partial(_mla_decode_body, pages=pages),
        out_shape=jax.ShapeDtypeStruct((nseq, MLA_HEADS, MLA_KV_RANK), f32),
        grid_spec=grid_spec,
        compiler_params=_cparams("parallel", "arbitrary"),
        name=name,
    )(page_table, q3, kn3, *([cache_ckv] * pages), *([kpe_t] * pages))
    return lat.reshape(nseq, MLA_HEADS * MLA_KV_RANK)


def _sb_decode_body(*refs, pages):
    pt_ref, q_ref, upper_ref = refs[:3]
    k_refs = refs[3:3 + pages]
    v_refs = refs[3 + pages:3 + 2 * pages]
    o_ref, qb_ref, acc_ref, run_ref = refs[3 + 2 * pages:]
    j = pl.program_id(1)
    n = SB_HEAD
    eye = (lax.broadcasted_iota(jnp.int32, (n, n), 0) == lax.broadcasted_iota(jnp.int32, (n, n), 1)).astype(f32)

    @pl.when(j == 0)
    def _():
        q = q_ref[0]
        qcol = jnp.sum(eye[None] * q[:, None, :], axis=-1, keepdims=True)
        qb_ref[...] = jnp.broadcast_to(qcol, qb_ref.shape)
        acc_ref[...] = jnp.zeros_like(acc_ref)
        run_ref[...] = jnp.zeros_like(run_ref)

    qb = qb_ref[...]
    z = jnp.concatenate([jnp.sum(qb * k_ref[0], axis=1) for k_ref in k_refs], axis=0) * SB_SCALE
    log_fail = -_softplus(z)
    after = _split_dot_right(log_fail, upper_ref[...])
    total = jnp.sum(log_fail, axis=-1, keepdims=True)
    run = run_ref[...]
    ws = []
    for p in range(pages):
        rows = slice(p * SB_HEADS, (p + 1) * SB_HEADS)
        ws.append(jnp.exp(z[rows] + log_fail[rows] + after[rows] + run))
        run = run + total[rows]
    run_ref[...] = run
    for h in range(SB_HEADS):
        a = acc_ref[h]
        for p in range(pages):
            a = a + v_refs[p][0, h] * ws[p][h:h + 1, :]
        acc_ref[h] = a

    @pl.when(j == pl.num_programs(1) - 1)
    def _():
        ocol = jnp.sum(acc_ref[...], axis=-1, keepdims=True)
        o_ref[0] = jnp.sum(eye[None] * ocol, axis=1)


def sb_decode(q, cache_k, cache_v, page_table, *, pages=16, name):
    nseq, npages = page_table.shape
    pages = min(pages, npages)
    assert npages % pages == 0
    ck = jnp.transpose(cache_k, (0, 2, 3, 1))
    cv = jnp.transpose(cache_v, (0, 2, 3, 1))
    q3 = q.reshape(nseq, SB_HEADS, SB_HEAD)
    upper = (lax.broadcasted_iota(jnp.int32, (PAGE_SIZE, PAGE_SIZE), 0)
             > lax.broadcasted_iota(jnp.int32, (PAGE_SIZE, PAGE_SIZE), 1)).astype(bf16)
    page = lambda i: pl.BlockSpec(
        (1, SB_HEADS, SB_HEAD, PAGE_SIZE), lambda b, j, pt, i=i: (pt[b, npages - 1 - (j * pages + i)], 0, 0, 0))
    state = pltpu.VMEM((SB_HEADS, SB_HEAD, PAGE_SIZE), f32)
    grid_spec = pltpu.PrefetchScalarGridSpec(
        num_scalar_prefetch=1,
        grid=(nseq, npages // pages),
        in_specs=[pl.BlockSpec((1, SB_HEADS, SB_HEAD), lambda b, j, pt: (b, 0, 0)),
                  pl.BlockSpec((PAGE_SIZE, PAGE_SIZE), lambda b, j, pt: (0, 0))]
        + [page(i) for i in range(pages)] * 2,
        out_specs=pl.BlockSpec((1, SB_HEADS, SB_HEAD), lambda b, j, pt: (b, 0, 0)),
        scratch_shapes=[state, state, pltpu.VMEM((SB_HEADS, 1), f32)],
    )
    out = pl.pallas_call(
        functools.partial(_sb_decode_body, pages=pages),
        out_shape=jax.ShapeDtypeStruct((nseq, SB_HEADS, SB_HEAD), f32),
        grid_spec=grid_spec,
        compiler_params=_cparams("parallel", "arbitrary"),
        name=name,
    )(page_table, q3, upper, *([ck] * pages), *([cv] * pages))
    return out.reshape(nseq, SB_W)


def _place(width, pieces, dtype=None):
    if dtype is not None:
        pieces = [(off, arr.astype(dtype)) for off, arr in pieces]
    lead = pieces[0][1].shape[:-1]
    out, pos = [], 0
    for off, arr in pieces:
        if off > pos:
            out.append(jnp.zeros(lead + (off - pos,), arr.dtype))
        out.append(arr)
        pos = off + arr.shape[-1]
    if width > pos:
        out.append(jnp.zeros(lead + (width - pos,), pieces[0][1].dtype))
    return jnp.concatenate(out, axis=-1)


def _rot_half_cols(w):
    half = w.shape[-1] // 2
    return jnp.concatenate([-w[..., half:], w[..., :half]], axis=-1)


def _rwkv_col_pieces(x):
    w = RWKV_W
    return [(C0_R, x[..., :3 * w]), (C0_WDAD, x[..., 3 * w:3 * w + 128]), (C0_GD, x[..., 3 * w + 128:])]


def _prepare(p):
    out = {}
    mla_cols = 2 * MLA_Q_RANK + MLA_ROPE
    w_in0 = p["w_in0"]
    w_kpe = w_in0[:, 2 * MLA_Q_RANK:mla_cols]
    out["w_in0"] = _place(C0_N, _rwkv_col_pieces(w_in0[:, mla_cols:])[:1] + [
        (C0_QA, w_in0[:, :2 * MLA_Q_RANK]), (C0_KPE, w_kpe), (C0_KPEROT, _rot_half_cols(w_kpe)),
    ] + _rwkv_col_pieces(w_in0[:, mla_cols:])[1:], bf16)
    d_qk = MLA_NOPE + MLA_ROPE
    wqb = p["mla_w_qb"].reshape(MLA_Q_RANK, MLA_HEADS, d_qk)
    nope = wqb[:, :, :MLA_NOPE]
    rope = wqb[:, :, MLA_NOPE:]
    pad = jnp.zeros((MLA_Q_RANK, MLA_HEADS, LANES - MLA_ROPE), f32)
    flat = lambda x: x.reshape(MLA_Q_RANK, -1)
    out["wq_all"] = jnp.concatenate(
        [flat(nope), flat(jnp.concatenate([rope, pad], -1)), flat(jnp.concatenate([_rot_half_cols(rope), pad], -1))],
        axis=1).astype(bf16)
    out["w_kv"] = jnp.concatenate([p["mla_w_uk"].reshape(MLA_KV_RANK, -1), p["mla_w_uv"].reshape(MLA_KV_RANK, -1)],
                                  axis=1).astype(bf16)
    out["w_uk_t"] = jnp.transpose(p["mla_w_uk"], (1, 2, 0)).astype(bf16)
    out["w_uv_h"] = jnp.transpose(p["mla_w_uv"], (1, 0, 2)).astype(bf16)
    mu = _place(C0_N, _rwkv_col_pieces(p["rwkv_mu"].reshape(1, -1)))
    zrows = lambda n: jnp.zeros((n, RWKV_W), f32)
    head_id = jnp.arange(RWKV_W, dtype=jnp.int32) // RWKV_HEAD
    out["rwkv"] = {
        "mu_r": mu[:, C0_R:C0_R + RWKV_W], "mu_k": mu[:, C0_K:C0_K + RWKV_W], "mu_v": mu[:, C0_V:C0_V + RWKV_W],
        "mu_wdad": mu[:, C0_WDAD:C0_WDAD + LANES], "mu_gd": mu[:, C0_GD:C0_GD + 2 * LANES],
        "w0": p["rwkv_w0"].reshape(1, -1), "a0": p["rwkv_a0"].reshape(1, -1),
        "w2p": jnp.concatenate([p["rwkv_w2"], zrows(LANES - RWKV_DECAY_RANK)], 0).astype(bf16),
        "a2p": jnp.concatenate([zrows(RWKV_DECAY_RANK), p["rwkv_a2"]], 0).astype(bf16),
        "g2p": jnp.concatenate([p["rwkv_g2"], zrows(2 * LANES - RWKV_GATE_RANK)], 0).astype(bf16),
        "k_k": p["rwkv_k_k"].reshape(1, -1), "k_a": p["rwkv_k_a"].reshape(1, -1),
        "head_ones": (head_id[:, None] == head_id[None, :]).astype(bf16),
        "r_k": p["rwkv_r_k"].reshape(1, -1), "ln_w": p["rwkv_ln_w"].reshape(1, -1), "ln_b": p["rwkv_ln_b"].reshape(1, -1),
    }
    w_in1 = p["w_in1"]
    ggd0 = 3 * SB_W + 2 * GLA_HEADS * GLA_DK + GLA_HEADS * GLA_DV
    out["w_in1"] = _place(C1_N, [(0, w_in1[:, :ggd0]), (C1_GOG, w_in1[:, ggd0 + GLA_GATE_RANK:]),
                                 (C1_GGD, w_in1[:, ggd0:ggd0 + GLA_GATE_RANK])], bf16)
    out["gla"] = {
        "wgup": jnp.concatenate([p["gla_w_gup"], jnp.zeros((LANES - GLA_GATE_RANK, GLA_HEADS * GLA_DK), f32)],
                                0).astype(bf16),
        "bg": p["gla_b_g"].reshape(1, -1), "gn": p["gla_norm"].reshape(1, -1),
    }
    for name in ("w_out0", "w_out1", "ffn_w_gate", "ffn_w_up", "ffn_w_down"):
        out[name] = p[name].astype(bf16)
    return out


def _rope_tables(pos):
    half = MLA_ROPE // 2
    inv = 1.0 / (ROPE_THETA ** (jnp.arange(half, dtype=f32) / half))
    ang = pos.astype(f32)[:, None] * inv[None, :]
    cos, sin = jnp.cos(ang), jnp.sin(ang)
    return jnp.concatenate([cos] * 4, axis=1), jnp.concatenate([sin] * 4, axis=1)


def _rwkv_shift_row(cols0_row):
    return jnp.concatenate([cols0_row[..., C0_R:C0_R + 3 * RWKV_W], cols0_row[..., C0_WDAD:C0_WDAD + LANES],
                            cols0_row[..., C0_GD:C0_GD + RWKV_GATE_RANK]], axis=-1)


def _trunk(x, pos, past, p, w, batch, seq, tag):
    m = batch * seq
    cos, sin = _rope_tables(pos)
    cos, sin = jnp.tile(cos, (batch, 1)), jnp.tile(sin, (batch, 1))
    state = {}
    tmm = 1024 if m >= 1024 else m

    cols0 = matmul([(x, x.shape[1], 0)], [w["w_in0"]], gain=p["norm_mix"][0], tm=tmm, name=f"{tag}_in0")
    if past is None:
        qcat, kcat, v, ckv, kpe = mla_prep(cols0, cos, sin, p["mla_q_norm"], w["wq_all"], p["mla_kv_norm"],
                                           w["w_kv"], absorbed=False, name=f"{tag}_mla_prep")
        a_out = mla_flash(qcat, kcat, v, batch, seq, name=f"{tag}_mla_attn")
        shift_p = jnp.zeros((batch, C0_N), f32)
        s0 = jnp.zeros((batch, RWKV_HEADS, RWKV_HEAD, RWKV_HEAD), f32)
        gates = rwkv_gates(cols0, shift_p, seq, w["rwkv"], name=f"{tag}_rwkv_gates")
        b_out, s_rwkv = rwkv_scan(gates, s0, w["rwkv"], batch, seq, name=f"{tag}_rwkv_scan")
    else:
        qcat, ckv, kpe, knew = mla_prep(cols0, cos, sin, p["mla_q_norm"], w["wq_all"], p["mla_kv_norm"],
                                        w["w_uk_t"], absorbed=True, name=f"{tag}_mla_prep")
        lat = mla_decode(qcat, knew, past["mla_ckv"], past["mla_kpe"], past["page_table"], name=f"{tag}_mla_attn")
        a_out = head_matmul(lat, w["w_uv_h"], name=f"{tag}_mla_uv")
        shift_p = _place(C0_N, _rwkv_col_pieces(past["rwkv_shift"]))
        gates = rwkv_gates(cols0, shift_p, 1, w["rwkv"], name=f"{tag}_rwkv_gates")
        b_out, s_rwkv = rwkv_step(gates, past["rwkv_state"], w["rwkv"], name=f"{tag}_rwkv_step")
    state.update(mla_ckv=ckv.reshape(batch, seq, MLA_KV_RANK), mla_kpe=kpe.reshape(batch, seq, MLA_ROPE),
                 rwkv_state=s_rwkv, rwkv_shift=_rwkv_shift_row(cols0.reshape(batch, seq, C0_N)[:, -1]))
    half = w["w_out0"].shape[0] // 2
    x = matmul([(a_out, half, 0), (b_out, half, 0)], [w["w_out0"][:half], w["w_out0"][half:]], res=x, tm=tmm,
               name=f"{tag}_out0")
    x = ffn(x, p["norm_ffn"][0], w["ffn_w_gate"][0], w["ffn_w_up"][0], w["ffn_w_down"][0], name=f"{tag}_ffn0")

    cols1 = matmul([(x, x.shape[1], 0)], [w["w_in1"]], gain=p["norm_mix"][1], tm=tmm, name=f"{tag}_in1")
    if past is None:
        c_out = sb_flash(cols1, batch, seq, name=f"{tag}_sb_attn")
        g0 = jnp.zeros((batch, GLA_HEADS, GLA_DK, GLA_DV), f32)
        d_out, s_gla = gla_scan(cols1, g0, w["gla"], batch, seq, name=f"{tag}_gla_scan")
    else:
        c_out = sb_decode(cols1[:, C1_SQ:C1_SQ + SB_W], past["sb_k"], past["sb_v"], past["page_table"],
                          name=f"{tag}_sb_attn")
        la = gla_gate(cols1, w["gla"], name=f"{tag}_gla_gate")
        kw, vw = GLA_HEADS * GLA_DK, GLA_HEADS * GLA_DV
        d_out, s_gla = gla_step(cols1[:, C1_GQ:C1_GQ + kw], cols1[:, C1_GK:C1_GK + kw], la,
                                cols1[:, C1_GV:C1_GV + vw], cols1[:, C1_GOG:C1_GOG + vw], past["gla_state"],
                                w["gla"], name=f"{tag}_gla_step")
    if past is None:
        k_t, v_t = kv_token_minor(cols1, batch, seq, name=f"{tag}_sb_kv")
        perm = (0, 3, 1, 2)
    else:
        k_t, v_t = kv_token_minor(cols1, 1, batch, name=f"{tag}_sb_kv")
        perm = (3, 0, 1, 2)
    state.update(sb_k=jnp.transpose(k_t, perm), sb_v=jnp.transpose(v_t, perm), gla_state=s_gla)
    half = w["w_out1"].shape[0] // 2
    x = matmul([(c_out, half, 0), (d_out, half, 0)], [w["w_out1"][:half], w["w_out1"][half:]], res=x, tm=tmm,
               name=f"{tag}_out1")
    y = ffn(x, p["norm_ffn"][1], w["ffn_w_gate"][1], w["ffn_w_up"][1], w["ffn_w_down"][1],
            final_gain=p["norm_final"], name=f"{tag}_ffn1")
    return y.reshape(batch, seq, -1), state


def kernel(x_prompt, x_sample, cache_mla_ckv, cache_mla_kpe, cache_sb_k, cache_sb_v, state_rwkv, state_rwkv_shift, state_gla, page_table, w_in0, mla_q_norm, mla_w_qb, mla_kv_norm, mla_w_uk, mla_w_uv, rwkv_mu, rwkv_w0, rwkv_w2, rwkv_a0, rwkv_a2, rwkv_g2, rwkv_k_k, rwkv_k_a, rwkv_r_k, rwkv_ln_w, rwkv_ln_b, w_out0, w_in1, gla_w_gup, gla_b_g, gla_norm, w_out1, norm_mix, norm_ffn, ffn_w_gate, ffn_w_up, ffn_w_down, norm_final):
    p = dict(w_in0=w_in0, mla_q_norm=mla_q_norm, mla_w_qb=mla_w_qb, mla_kv_norm=mla_kv_norm, mla_w_uk=mla_w_uk,
             mla_w_uv=mla_w_uv, rwkv_mu=rwkv_mu, rwkv_w0=rwkv_w0, rwkv_w2=rwkv_w2, rwkv_a0=rwkv_a0,
             rwkv_a2=rwkv_a2, rwkv_g2=rwkv_g2, rwkv_k_k=rwkv_k_k, rwkv_k_a=rwkv_k_a, rwkv_r_k=rwkv_r_k,
             rwkv_ln_w=rwkv_ln_w, rwkv_ln_b=rwkv_ln_b, w_out0=w_out0, w_in1=w_in1, gla_w_gup=gla_w_gup,
             gla_b_g=gla_b_g, gla_norm=gla_norm, w_out1=w_out1, norm_mix=norm_mix, norm_ffn=norm_ffn,
             ffn_w_gate=ffn_w_gate, ffn_w_up=ffn_w_up, ffn_w_down=ffn_w_down, norm_final=norm_final)
    w = _prepare(p)
    b, t, d = x_prompt.shape
    db, dt, _ = x_sample.shape
    assert dt == 1
    past_len = page_table.shape[1] * PAGE_SIZE
    y_p, sp = _trunk(x_prompt.reshape(b * t, d), jnp.arange(t, dtype=jnp.int32), None, p, w, b, t, "p")
    past = dict(mla_ckv=cache_mla_ckv, mla_kpe=cache_mla_kpe, sb_k=cache_sb_k, sb_v=cache_sb_v,
                rwkv_state=state_rwkv, rwkv_shift=state_rwkv_shift, gla_state=state_gla, page_table=page_table)
    y_s, ss = _trunk(x_sample.reshape(db, d), past_len + jnp.arange(1, dtype=jnp.int32), past, p, w, db, 1, "s")
    keys = ("mla_ckv", "mla_kpe", "rwkv_state", "rwkv_shift", "sb_k", "sb_v", "gla_state")
    return (y_p, y_s) + tuple(sp[k] for k in keys) + tuple(ss[k] for k in keys)
```

```python
import functools
import math

import jax
import jax.numpy as jnp
from jax import lax
from jax.experimental import pallas as pl
from jax.experimental.pallas import tpu as pltpu

f32 = jnp.float32
bf16 = jnp.bfloat16

PAGE_SIZE = 128
MLA_HEADS = 8
MLA_NOPE = 128
MLA_ROPE = 64
MLA_V = 128
MLA_Q_RANK = 512
MLA_KV_RANK = 512
MLA_SCALE = (MLA_NOPE + MLA_ROPE) ** -0.5
ROPE_THETA = 10000.0
RWKV_HEADS = 16
RWKV_HEAD = 64
RWKV_W = RWKV_HEADS * RWKV_HEAD
RWKV_DECAY_RANK = 64
RWKV_A_RANK = 64
RWKV_GATE_RANK = 160
RWKV_GN_EPS = 64e-5
RWKV_COLS = 3 * RWKV_W + RWKV_DECAY_RANK + RWKV_A_RANK + RWKV_GATE_RANK
SB_HEADS = 16
SB_HEAD = 64
SB_W = SB_HEADS * SB_HEAD
SB_SCALE = SB_HEAD ** -0.5
GLA_HEADS = 4
GLA_DK = 128
GLA_DV = 256
GLA_GATE_RANK = 16
GLA_GATE_NORM = 16.0
NORM_EPS = 1e-6

LANES = 128
MXU_TILE = 256
VMEM_LIMIT_BYTES = 56 * 1024 * 1024

C0_R, C0_K, C0_V = 0, 1024, 2048
C0_QA, C0_KVA = 3072, 3584
C0_KPE, C0_KPEROT = 4096, 4224
C0_WDAD = 4352
C0_GD = 4608
C0_N = 5120
C1_SQ, C1_SK, C1_SV = 0, 1024, 2048
C1_GQ, C1_GK, C1_GV = 3072, 3584, 4096
C1_GOG = 5120
C1_GGD = 6144
C1_N = 6656

RWKV_CHUNK = 64
GLA_CHUNK = 64
MLA_SLOT = 2 * LANES
ABS_SLOT = MLA_KV_RANK + LANES


def _cparams(*sem):
    return pltpu.CompilerParams(dimension_semantics=sem, vmem_limit_bytes=VMEM_LIMIT_BYTES)


def _dot(a, b):
    return jnp.dot(a, b, preferred_element_type=f32)


def _dot_nt(a, b):
    return lax.dot_general(a, b, (((1,), (1,)), ((), ())), preferred_element_type=f32)


def _dot_tn(a, b):
    return lax.dot_general(a, b, (((0,), (0,)), ((), ())), preferred_element_type=f32)


def _split_dot_right(x, m):
    hi = x.astype(bf16)
    lo = (x - hi.astype(f32)).astype(bf16)
    return _dot(hi, m) + _dot(lo, m)


def _split_dot_left(m, x):
    hi = x.astype(bf16)
    lo = (x - hi.astype(f32)).astype(bf16)
    return _dot(m, hi) + _dot(m, lo)


def _sigmoid(x):
    return 1.0 / (1.0 + jnp.exp(-x))


def _softplus(x):
    return jnp.maximum(x, 0.0) + jnp.log(1.0 + jnp.exp(-jnp.abs(x)))


def _rms(x, gain):
    return x * lax.rsqrt(jnp.mean(x * x, axis=-1, keepdims=True) + NORM_EPS) * gain


def _mm_body(*refs, n_lhs, norm, res):
    lhs = refs[:n_lhs]
    ws = refs[n_lhs:2 * n_lhs]
    pos = 2 * n_lhs
    g_ref = refs[pos] if norm else None
    pos += int(norm)
    r_ref = refs[pos] if res else None
    pos += int(res)
    o_ref = refs[pos]
    hs = refs[pos + 1:pos + 1 + n_lhs]

    @pl.when(pl.program_id(1) == 0)
    def _():
        for i in range(n_lhs):
            x = lhs[i][...].astype(f32)
            if norm and i == 0:
                x = _rms(x, g_ref[...])
            hs[i][...] = x.astype(bf16)

    acc = _dot(hs[0][...], ws[0][...])
    for i in range(1, n_lhs):
        acc = acc + _dot(hs[i][...], ws[i][...])
    if res:
        acc = acc + r_ref[...]
    o_ref[...] = acc.astype(o_ref.dtype)


def matmul(lhs_list, w_list, *, gain=None, res=None, out_dtype=f32, tm=512, tn=512, name):
    m = lhs_list[0][0].shape[0]
    n = w_list[0].shape[1]
    tm, tn = min(tm, m), min(tn, n)
    assert m % tm == 0 and n % tn == 0
    in_specs, args, scratch = [], [], []
    for arr, k, cb in lhs_list:
        in_specs.append(pl.BlockSpec((tm, k), lambda i, j, cb=cb: (i, cb)))
        args.append(arr)
        scratch.append(pltpu.VMEM((tm, k), bf16))
    for (arr, k, cb), w in zip(lhs_list, w_list):
        assert w.shape[0] == k
        in_specs.append(pl.BlockSpec((k, tn), lambda i, j: (0, j)))
        args.append(w)
    if gain is not None:
        in_specs.append(pl.BlockSpec((1, lhs_list[0][1]), lambda i, j: (0, 0)))
        args.append(gain.reshape(1, -1))
    if res is not None:
        in_specs.append(pl.BlockSpec((tm, tn), lambda i, j: (i, j)))
        args.append(res)
    body = functools.partial(_mm_body, n_lhs=len(lhs_list), norm=gain is not None, res=res is not None)
    return pl.pallas_call(
        body,
        out_shape=jax.ShapeDtypeStruct((m, n), out_dtype),
        grid=(m // tm, n // tn),
        in_specs=in_specs,
        out_specs=pl.BlockSpec((tm, tn), lambda i, j: (i, j)),
        scratch_shapes=scratch,
        compiler_params=_cparams("parallel", "arbitrary"),
        name=name,
    )(*args)


def _hmm_body(x_ref, w_ref, o_ref):
    o_ref[...] = _dot(x_ref[...].astype(bf16), w_ref[0]).astype(o_ref.dtype)


def head_matmul(x, w, *, name):
    m = x.shape[0]
    h, k, n = w.shape
    return pl.pallas_call(
        _hmm_body,
        out_shape=jax.ShapeDtypeStruct((m, h * n), f32),
        grid=(h,),
        in_specs=[pl.BlockSpec((m, k), lambda i: (0, i)), pl.BlockSpec((1, k, n), lambda i: (i, 0, 0))],
        out_specs=pl.BlockSpec((m, n), lambda i: (0, i)),
        compiler_params=_cparams("arbitrary"),
        name=name,
    )(x, w)


def _ffn_body(*refs, final):
    x_ref, g_ref, wg_ref, wu_ref, wd_ref = refs[:5]
    gf_ref = refs[5] if final else None
    o_ref, h_ref, acc_ref = refs[5 + int(final):]
    f = pl.program_id(1)

    @pl.when(f == 0)
    def _():
        h_ref[...] = _rms(x_ref[...], g_ref[...]).astype(bf16)
        acc_ref[...] = jnp.zeros_like(acc_ref)

    h = h_ref[...]
    a = _dot(h, wg_ref[...])
    u = _dot(h, wu_ref[...])
    s = (a * _sigmoid(a) * u).astype(bf16)
    acc_ref[...] += _dot(s, wd_ref[...])

    @pl.when(f == pl.num_programs(1) - 1)
    def _():
        y = x_ref[...] + acc_ref[...]
        if final:
            y = _rms(y, gf_ref[...])
        o_ref[...] = y


def ffn(x, gain, wg, wu, wd, *, final_gain=None, tm=512, tf=512, name):
    m, d = x.shape
    dff = wg.shape[1]
    tm = min(tm, m)
    assert m % tm == 0 and dff % tf == 0
    in_specs = [
        pl.BlockSpec((tm, d), lambda i, j: (i, 0)),
        pl.BlockSpec((1, d), lambda i, j: (0, 0)),
        pl.BlockSpec((d, tf), lambda i, j: (0, j)),
        pl.BlockSpec((d, tf), lambda i, j: (0, j)),
        pl.BlockSpec((tf, d), lambda i, j: (j, 0)),
    ]
    args = [x, gain.reshape(1, d), wg, wu, wd]
    if final_gain is not None:
        in_specs.append(pl.BlockSpec((1, d), lambda i, j: (0, 0)))
        args.append(final_gain.reshape(1, d))
    return pl.pallas_call(
        functools.partial(_ffn_body, final=final_gain is not None),
        out_shape=jax.ShapeDtypeStruct((m, d), f32),
        grid=(m // tm, dff // tf),
        in_specs=in_specs,
        out_specs=pl.BlockSpec((tm, d), lambda i, j: (i, 0)),
        scratch_shapes=[pltpu.VMEM((tm, d), bf16), pltpu.VMEM((tm, d), f32)],
        compiler_params=_cparams("parallel", "arbitrary"),
        name=name,
    )(*args)


def _mla_prep_body(*refs, absorbed):
    (qa_ref, kva_ref, kpe_ref, kperot_ref, cos_ref, sin_ref, qg_ref, wq_ref, kg_ref, wkv_ref) = refs[:10]
    outs = refs[10:]
    cos, sin = cos_ref[...], sin_ref[...]
    qn = _rms(qa_ref[...], qg_ref[...]).astype(bf16)
    qall = _dot(qn, wq_ref[...])
    hw = MLA_HEADS * LANES
    nope, pe, rot = qall[:, :hw], qall[:, hw:2 * hw], qall[:, 2 * hw:]
    cos8 = jnp.concatenate([cos] * MLA_HEADS, axis=1)
    sin8 = jnp.concatenate([sin] * MLA_HEADS, axis=1)
    roped = pe * cos8 + rot * sin8
    ckv = _rms(kva_ref[...], kg_ref[...])
    kpe = kpe_ref[...] * cos + kperot_ref[...] * sin
    if absorbed:
        qcat_ref, ckv_ref, kpe_out_ref, knew_ref = outs
        pieces = []
        for h in range(MLA_HEADS):
            qabs = _dot(nope[:, h * LANES:(h + 1) * LANES].astype(bf16), wkv_ref[h])
            pieces += [qabs, roped[:, h * LANES:(h + 1) * LANES]]
        qcat_ref[...] = jnp.concatenate(pieces, axis=1).astype(bf16)
        knew_ref[...] = jnp.concatenate([ckv, kpe], axis=1)
    else:
        qcat_ref, kcat_ref, v_ref, ckv_ref, kpe_out_ref = outs
        kv = _dot(ckv.astype(bf16), wkv_ref[...])
        qp, kp = [], []
        for h in range(MLA_HEADS):
            sl = slice(h * LANES, (h + 1) * LANES)
            qp += [nope[:, sl], roped[:, sl]]
            kp += [kv[:, sl], kpe]
        qcat_ref[...] = jnp.concatenate(qp, axis=1).astype(bf16)
        kcat_ref[...] = jnp.concatenate(kp, axis=1).astype(bf16)
        v_ref[...] = kv[:, hw:].astype(bf16)
    ckv_ref[...] = ckv
    kpe_out_ref[...] = kpe[:, :MLA_ROPE]


def mla_prep(cols0, cos, sin, q_gain, wq_all, kv_gain, wkv, *, absorbed, tm=256, name):
    m = cols0.shape[0]
    tm = min(tm, m)
    assert m % tm == 0
    row = lambda w, cb: pl.BlockSpec((tm, w), lambda i, cb=cb: (i, cb))
    full = lambda a: pl.BlockSpec(a.shape, lambda i, nd=a.ndim: (0,) * nd)
    qg, kg = q_gain.reshape(1, -1), kv_gain.reshape(1, -1)
    in_specs = [
        row(MLA_Q_RANK, C0_QA // MLA_Q_RANK), row(MLA_KV_RANK, C0_KVA // MLA_KV_RANK),
        row(LANES, C0_KPE // LANES), row(LANES, C0_KPEROT // LANES),
        row(LANES, 0), row(LANES, 0), full(qg), full(wq_all), full(kg), full(wkv),
    ]
    if absorbed:
        out_shape = [jax.ShapeDtypeStruct((m, MLA_HEADS * ABS_SLOT), bf16),
                     jax.ShapeDtypeStruct((m, MLA_KV_RANK), f32),
                     jax.ShapeDtypeStruct((m, MLA_ROPE), f32),
                     jax.ShapeDtypeStruct((m, ABS_SLOT), f32)]
        out_specs = [row(MLA_HEADS * ABS_SLOT, 0), row(MLA_KV_RANK, 0), row(MLA_ROPE, 0), row(ABS_SLOT, 0)]
    else:
        out_shape = [jax.ShapeDtypeStruct((m, MLA_HEADS * MLA_SLOT), bf16),
                     jax.ShapeDtypeStruct((m, MLA_HEADS * MLA_SLOT), bf16),
                     jax.ShapeDtypeStruct((m, MLA_HEADS * MLA_V), bf16),
                     jax.ShapeDtypeStruct((m, MLA_KV_RANK), f32),
                     jax.ShapeDtypeStruct((m, MLA_ROPE), f32)]
        out_specs = [row(MLA_HEADS * MLA_SLOT, 0), row(MLA_HEADS * MLA_SLOT, 0), row(MLA_HEADS * MLA_V, 0),
                     row(MLA_KV_RANK, 0), row(MLA_ROPE, 0)]
    return pl.pallas_call(
        functools.partial(_mla_prep_body, absorbed=absorbed),
        out_shape=out_shape,
        grid=(m // tm,),
        in_specs=in_specs,
        out_specs=out_specs,
        compiler_params=_cparams("parallel"),
        name=name,
    )(cols0, cols0, cols0, cols0, cos, sin, qg, wq_all, kg, wkv)


def _causal_pairs(nblk, reverse):
    qi, kj = [], []
    for i in range(nblk):
        ks = range(i, -1, -1) if reverse else range(i + 1)
        for j in ks:
            qi.append(i)
            kj.append(j)
    return jnp.asarray(qi, jnp.int32), jnp.asarray(kj, jnp.int32)


def _mla_flash_body(qi_ref, kj_ref, q_ref, k_ref, v_ref, o_ref, m_ref, l_ref, acc_ref, *, hp):
    p = pl.program_id(2)
    i, j = qi_ref[p], kj_ref[p]
    heads = range(hp)

    @pl.when(j == 0)
    def _():
        m_ref[...] = jnp.full_like(m_ref, -jnp.inf)
        l_ref[...] = jnp.zeros_like(l_ref)
        acc_ref[...] = jnp.zeros_like(acc_ref)

    def step(diagonal):
        q, k, v = q_ref[...], k_ref[...], v_ref[...]
        qk = [slice(h * MLA_SLOT, (h + 1) * MLA_SLOT) for h in heads]
        s = [_dot_nt(q[:, qk[h]], k[:, qk[h]]) * MLA_SCALE for h in heads]
        if diagonal:
            keep = (lax.broadcasted_iota(jnp.int32, s[0].shape, 1) <= lax.broadcasted_iota(jnp.int32, s[0].shape, 0))
            s = [jnp.where(keep, x, -jnp.inf) for x in s]
        m_prev = [m_ref[h] for h in heads]
        m_new = [jnp.maximum(m_prev[h], jnp.max(s[h], axis=-1, keepdims=True)) for h in heads]
        alpha = [jnp.exp(m_prev[h] - m_new[h]) for h in heads]
        pr = [jnp.exp(s[h] - m_new[h]) for h in heads]
        for h in heads:
            l_ref[h] = l_ref[h] * alpha[h] + jnp.sum(pr[h], axis=-1, keepdims=True)
            acc_ref[h] = acc_ref[h] * alpha[h] + _dot(pr[h].astype(bf16), v[:, h * MLA_V:(h + 1) * MLA_V])
            m_ref[h] = m_new[h]

    @pl.when(j < i)
    def _():
        step(False)

    @pl.when(j == i)
    def _():
        step(True)
        o_ref[...] = jnp.concatenate([acc_ref[h] / l_ref[h] for h in heads], axis=1)


def mla_flash(qcat, kcat, v, batch, seq, *, tq=512, hp=4, name):
    tq = min(tq, seq)
    nblk = seq // tq
    qi, kj = _causal_pairs(nblk, reverse=False)
    grid_spec = pltpu.PrefetchScalarGridSpec(
        num_scalar_prefetch=2,
        grid=(batch, MLA_HEADS // hp, int(qi.shape[0])),
        in_specs=[
            pl.BlockSpec((tq, hp * MLA_SLOT), lambda b, h, p, qi, kj: (b * nblk + qi[p], h)),
            pl.BlockSpec((tq, hp * MLA_SLOT), lambda b, h, p, qi, kj: (b * nblk + kj[p], h)),
            pl.BlockSpec((tq, hp * MLA_V), lambda b, h, p, qi, kj: (b * nblk + kj[p], h)),
        ],
        out_specs=pl.BlockSpec((tq, hp * MLA_V), lambda b, h, p, qi, kj: (b * nblk + qi[p], h)),
        scratch_shapes=[pltpu.VMEM((hp, tq, 1), f32), pltpu.VMEM((hp, tq, 1), f32),
                        pltpu.VMEM((hp, tq, MLA_V), f32)],
    )
    return pl.pallas_call(
        functools.partial(_mla_flash_body, hp=hp),
        out_shape=jax.ShapeDtypeStruct((batch * seq, MLA_HEADS * MLA_V), f32),
        grid_spec=grid_spec,
        compiler_params=_cparams("parallel", "parallel", "arbitrary"),
        name=name,
    )(qi, kj, qcat, kcat, v)


def _sb_flash_body(qi_ref, kj_ref, q_ref, k_ref, v_ref, u_ref, o_ref, acc_ref, run_ref, *, tq, hp):
    p = pl.program_id(2)
    i, j = qi_ref[p], kj_ref[p]

    heads = range(hp)

    def step(diagonal):
        q = (q_ref[...] * SB_SCALE).astype(bf16)
        k, v = k_ref[...].astype(bf16), v_ref[...].astype(bf16)
        upper = u_ref[...]
        sls = [slice(h * SB_HEAD, (h + 1) * SB_HEAD) for h in heads]
        z = [_dot_nt(q[:, sl], k[:, sl]) for sl in sls]
        log_fail = [-_softplus(x) for x in z]
        if diagonal:
            valid = lax.broadcasted_iota(jnp.int32, z[0].shape, 1) < lax.broadcasted_iota(jnp.int32, z[0].shape, 0)
            log_fail = [jnp.where(valid, x, 0.0) for x in log_fail]
        sw = upper.shape[0]
        nsub = z[0].shape[1] // sw
        after, total = [], []
        for h in heads:
            parts = [log_fail[h][:, c * sw:(c + 1) * sw] for c in range(nsub)]
            sums = [jnp.sum(x, axis=-1, keepdims=True) for x in parts]
            local = [_split_dot_right(x, upper) for x in parts]
            later = sums[nsub - 1]
            for c in range(nsub - 2, -1, -1):
                local[c] = local[c] + later
                later = later + sums[c]
            after.append(jnp.concatenate(local, axis=1) if nsub > 1 else local[0])
            total.append(later)
        if diagonal:
            w = [jnp.where(valid, jnp.exp(z[h] + log_fail[h] + after[h]), 0.0) for h in heads]
        else:
            w = [jnp.exp(z[h] + log_fail[h] + after[h] + run_ref[h]) for h in heads]
        for h in heads:
            part = _dot(w[h].astype(bf16), v[:, sls[h]])
            if diagonal:
                acc_ref[h] = part
                run_ref[h] = total[h]
            else:
                acc_ref[h] += part
                run_ref[h] += total[h]

    @pl.when(j == i)
    def _():
        step(True)

    @pl.when(j < i)
    def _():
        step(False)

    @pl.when(j == 0)
    def _():
        o_ref[...] = jnp.concatenate([acc_ref[h] for h in heads], axis=1)


def sb_flash(cols1, batch, seq, *, tq=512, hp=4, name):
    tq = min(tq, seq)
    nblk = seq // tq
    qi, kj = _causal_pairs(nblk, reverse=True)
    w = hp * SB_HEAD
    nq, nk, nv = C1_SQ // w, C1_SK // w, C1_SV // w
    sw = min(tq, MXU_TILE)
    assert tq % sw == 0
    upper = (lax.broadcasted_iota(jnp.int32, (sw, sw), 0) > lax.broadcasted_iota(jnp.int32, (sw, sw), 1)).astype(bf16)
    grid_spec = pltpu.PrefetchScalarGridSpec(
        num_scalar_prefetch=2,
        grid=(batch, SB_HEADS // hp, int(qi.shape[0])),
        in_specs=[
            pl.BlockSpec((tq, w), lambda b, h, p, qi, kj: (b * nblk + qi[p], nq + h)),
            pl.BlockSpec((tq, w), lambda b, h, p, qi, kj: (b * nblk + kj[p], nk + h)),
            pl.BlockSpec((tq, w), lambda b, h, p, qi, kj: (b * nblk + kj[p], nv + h)),
            pl.BlockSpec((sw, sw), lambda b, h, p, qi, kj: (0, 0)),
        ],
        out_specs=pl.BlockSpec((tq, w), lambda b, h, p, qi, kj: (b * nblk + qi[p], h)),
        scratch_shapes=[pltpu.VMEM((hp, tq, SB_HEAD), f32), pltpu.VMEM((hp, tq, 1), f32)],
    )
    return pl.pallas_call(
        functools.partial(_sb_flash_body, tq=tq, hp=hp),
        out_shape=jax.ShapeDtypeStruct((batch * seq, SB_W), f32),
        grid_spec=grid_spec,
        compiler_params=_cparams("parallel", "parallel", "arbitrary"),
        name=name,
    )(qi, kj, cols1, cols1, cols1, upper)


def _rwkv_gates_body(*refs, seq_rows):
    (r_ref, k_ref, v_ref, wdad_ref, gd_ref) = refs[:5]
    pos = 5
    if seq_rows == 1:
        prevs = [ref[...] for ref in refs[pos:pos + 5]]
        pos += 5
    else:
        tails = refs[pos:pos + 5]
        firsts = refs[pos + 5:pos + 10]
        pos += 10
    (mu_r, mu_k, mu_v, mu_wdad, mu_gd, w0_ref, w2_ref, a0_ref, a2_ref, g2_ref, kk_ref, ka_ref, ones_ref) = refs[pos:pos + 13]
    (r_out, k_out, v_out, kkn_out, a_out, lw_out, g_out) = refs[pos + 13:]
    cur = [r_ref[...], k_ref[...], v_ref[...], wdad_ref[...], gd_ref[...]]
    if seq_rows != 1:
        tm = cur[0].shape[0]
        starts_seq = (pl.program_id(0) * tm) % seq_rows == 0
        prevs = []
        for x, tail, first in zip(cur, tails, firsts):
            carry = jnp.where(starts_seq, first[0], tail[7:8, :])
            rolled = pltpu.roll(x, 1, 0)
            rowid = lax.broadcasted_iota(jnp.int32, x.shape, 0)
            prevs.append(jnp.where(rowid == 0, carry, rolled))
    mus = [mu_r[...], mu_k[...], mu_v[...], mu_wdad[...], mu_gd[...]]
    xr, xk, xv, xwdad, xgd = [c + (p - c) * m for c, p, m in zip(cur, prevs, mus)]
    w_log = -_softplus(-(w0_ref[...] + _dot(jnp.tanh(xwdad).astype(bf16), w2_ref[...]))) - 0.5
    lw_out[...] = -jnp.exp(w_log)
    a = _sigmoid(a0_ref[...] + _dot(xwdad.astype(bf16), a2_ref[...]))
    g_out[...] = _dot(_sigmoid(xgd).astype(bf16), g2_ref[...])
    kk = xk * kk_ref[...]
    ssq = _split_dot_right(kk * kk, ones_ref[...])
    kkn_out[...] = kk / jnp.maximum(jnp.sqrt(ssq), 1e-12)
    k_out[...] = xk * (1.0 + (a - 1.0) * ka_ref[...])
    r_out[...] = xr
    v_out[...] = xv
    a_out[...] = a


def rwkv_gates(cols0, shift_p, seq_rows, prm, *, tm=256, name):
    m = cols0.shape[0]
    tm = min(tm, m, seq_rows) if seq_rows != 1 else min(tm, m)
    assert m % tm == 0 and (seq_rows == 1 or (seq_rows % tm == 0 and tm % 8 == 0))
    groups = [(RWKV_W, C0_R // RWKV_W), (RWKV_W, C0_K // RWKV_W), (RWKV_W, C0_V // RWKV_W),
              (LANES, C0_WDAD // LANES), (2 * LANES, C0_GD // (2 * LANES))]
    in_specs = [pl.BlockSpec((tm, w), lambda i, cb=cb: (i, cb)) for w, cb in groups]
    args = [cols0] * 5
    if seq_rows == 1:
        in_specs += [pl.BlockSpec((tm, w), lambda i, cb=cb: (i, cb)) for w, cb in groups]
        args += [shift_p] * 5
    else:
        per = tm // 8
        in_specs += [pl.BlockSpec((8, w), lambda i, cb=cb: (jnp.maximum(i * per - 1, 0), cb)) for w, cb in groups]
        args += [cols0] * 5
        shift3 = shift_p.reshape(shift_p.shape[0], 1, C0_N)
        in_specs += [pl.BlockSpec((1, 1, w), lambda i, cb=cb: ((i * tm) // seq_rows, 0, cb)) for w, cb in groups]
        args += [shift3] * 5
    small = [prm["mu_r"], prm["mu_k"], prm["mu_v"], prm["mu_wdad"], prm["mu_gd"], prm["w0"], prm["w2p"],
             prm["a0"], prm["a2p"], prm["g2p"], prm["k_k"], prm["k_a"], prm["head_ones"]]
    in_specs += [pl.BlockSpec(a.shape, lambda i: (0, 0)) for a in small]
    args += small
    out_spec = pl.BlockSpec((tm, RWKV_W), lambda i: (i, 0))
    return pl.pallas_call(
        functools.partial(_rwkv_gates_body, seq_rows=seq_rows),
        out_shape=[jax.ShapeDtypeStruct((m, RWKV_W), f32)] * 7,
        grid=(m // tm,),
        in_specs=in_specs,
        out_specs=[out_spec] * 7,
        compiler_params=_cparams("parallel"),
        name=name,
    )(*args)


def _rwkv_scan_body(r_ref, k_ref, v_ref, kk_ref, a_ref, lw_ref, g_ref, s0_ref, rk_ref, lnw_ref, lnb_ref,
                    y_ref, s_out_ref, s_ref):
    c = pl.program_id(1)
    C = r_ref.shape[0]

    @pl.when(c == 0)
    def _():
        s_ref[...] = s0_ref[0]

    rowi = lax.broadcasted_iota(jnp.int32, (C, C), 0)
    coli = lax.broadcasted_iota(jnp.int32, (C, C), 1)
    incl = coli <= rowi
    strict = coli < rowi
    lower = incl.astype(bf16)
    r, k, v, kk, a, lw, g = (x[...] for x in (r_ref, k_ref, v_ref, kk_ref, a_ref, lw_ref, g_ref))
    cum = _split_dot_left(lower, lw)
    gam = jnp.exp(cum)
    ginv = jnp.exp(-cum)
    a_m = -kk * jnp.exp(cum - lw)
    b_m = kk * a * ginv
    k_m = k * ginv
    r_m = r * gam
    cum_last = cum[C - 1:C, :]
    g_last = jnp.exp(cum_last)
    b_end = kk * a * jnp.exp(cum_last - cum)
    k_end = k * jnp.exp(cum_last - cum)
    bonus_w = r * k * rk_ref[...]
    lnw, lnb = lnw_ref[...], lnb_ref[...]
    heads = range(RWKV_HEADS)
    sls = [slice(h * RWKV_HEAD, (h + 1) * RWKV_HEAD) for h in heads]
    ar = [jnp.concatenate([a_m[:, sl], r_m[:, sl]], axis=0).astype(bf16) for sl in sls]
    vh = [v[:, sl].astype(bf16) for sl in sls]
    s_old = [s_ref[h] for h in heads]
    g_b = [_dot_nt(ar[h], b_m[:, sls[h]].astype(bf16)) for h in heads]
    g_k = [_dot_nt(ar[h], k_m[:, sls[h]].astype(bf16)) for h in heads]
    g_s = [_dot_nt(ar[h], s_old[h].astype(bf16)) for h in heads]
    x = [g_s[h][:C] + _dot(jnp.where(strict, g_k[h][:C], 0.0).astype(bf16), vh[h]) for h in heads]
    n = [jnp.where(strict, g_b[h][:C], 0.0).astype(bf16) for h in heads]
    steps = int(math.log2(C))
    for it in range(steps):
        x = [x[h] + _dot(n[h], x[h].astype(bf16)) for h in heads]
        if it + 1 < steps:
            n = [_dot(n[h], n[h]).astype(bf16) for h in heads]
    xb = [x[h].astype(bf16) for h in heads]
    y = [g_s[h][C:] + _dot(jnp.where(incl, g_b[h][C:], 0.0).astype(bf16), xb[h])
         + _dot(jnp.where(incl, g_k[h][C:], 0.0).astype(bf16), vh[h]) for h in heads]
    for h in heads:
        pv = jnp.concatenate([xb[h], vh[h]], axis=0)
        bk_end = jnp.concatenate([b_end[:, sls[h]], k_end[:, sls[h]]], axis=0).astype(bf16)
        s_ref[h] = s_old[h] * g_last[:, sls[h]] + _dot_tn(pv, bk_end)
    pieces = []
    for h in heads:
        sl = sls[h]
        mean = jnp.mean(y[h], axis=-1, keepdims=True)
        var = jnp.mean(jnp.square(y[h] - mean), axis=-1, keepdims=True)
        yn = (y[h] - mean) * lax.rsqrt(var + RWKV_GN_EPS) * lnw[:, sl] + lnb[:, sl]
        bonus = jnp.sum(bonus_w[:, sl], axis=-1, keepdims=True) * v[:, sl]
        pieces.append((yn + bonus) * g[:, sl])
    y_ref[...] = jnp.concatenate(pieces, axis=1)

    @pl.when(c == pl.num_programs(1) - 1)
    def _():
        s_out_ref[0] = s_ref[...]


def rwkv_scan(gates, s0, prm, batch, seq, *, name):
    C = min(RWKV_CHUNK, seq)
    nch = seq // C
    tok = pl.BlockSpec((C, RWKV_W), lambda b, c: (b * nch + c, 0))
    st = pl.BlockSpec((1, RWKV_HEADS, RWKV_HEAD, RWKV_HEAD), lambda b, c: (b, 0, 0, 0))
    vec = pl.BlockSpec((1, RWKV_W), lambda b, c: (0, 0))
    r, k, v, kkn, a, lw, g = gates
    return pl.pallas_call(
        _rwkv_scan_body,
        out_shape=[jax.ShapeDtypeStruct((batch * seq, RWKV_W), f32),
                   jax.ShapeDtypeStruct((batch, RWKV_HEADS, RWKV_HEAD, RWKV_HEAD), f32)],
        grid=(batch, nch),
        in_specs=[tok] * 7 + [st, vec, vec, vec],
        out_specs=[tok, st],
        scratch_shapes=[pltpu.VMEM((RWKV_HEADS, RWKV_HEAD, RWKV_HEAD), f32)],
        compiler_params=_cparams("parallel", "arbitrary"),
        name=name,
    )(r, k, v, kkn, a, lw, g, s0, prm["r_k"], prm["ln_w"], prm["ln_b"])


def _rwkv_step_body(r_ref, k_ref, v_ref, kk_ref, a_ref, lw_ref, g_ref, s0_ref, rk_ref, lnw_ref, lnb_ref,
                    y_ref, s_out_ref):
    n = RWKV_HEAD
    eye = (lax.broadcasted_iota(jnp.int32, (n, n), 0) == lax.broadcasted_iota(jnp.int32, (n, n), 1)).astype(f32)
    r, k, v, kk, a, lw, g = (x[...] for x in (r_ref, k_ref, v_ref, kk_ref, a_ref, lw_ref, g_ref))
    s = s0_ref[...]
    sa = jnp.sum(s * (-kk), axis=-1, keepdims=True)
    vcol = jnp.sum(eye * v, axis=-1, keepdims=True)
    s = s * jnp.exp(lw) + sa * (kk * a) + vcol * k
    s_out_ref[...] = s
    ycol = jnp.sum(s * r, axis=-1, keepdims=True)
    y = jnp.sum(eye * ycol, axis=-2, keepdims=True)
    mean = jnp.mean(y, axis=-1, keepdims=True)
    var = jnp.mean(jnp.square(y - mean), axis=-1, keepdims=True)
    y = (y - mean) * lax.rsqrt(var + RWKV_GN_EPS) * lnw_ref[...] + lnb_ref[...]
    bonus = jnp.sum(r * k * rk_ref[...], axis=-1, keepdims=True) * v
    y_ref[...] = (y + bonus) * g


def rwkv_step(gates, s0, prm, *, bs=8, name):
    m = s0.shape[0]
    bs = min(bs, m)
    assert m % bs == 0
    h4 = lambda x: x.reshape(-1, RWKV_HEADS, 1, RWKV_HEAD)
    vec = pl.BlockSpec((bs, RWKV_HEADS, 1, RWKV_HEAD), lambda i: (i, 0, 0, 0))
    st = pl.BlockSpec((bs, RWKV_HEADS, RWKV_HEAD, RWKV_HEAD), lambda i: (i, 0, 0, 0))
    par = pl.BlockSpec((1, RWKV_HEADS, 1, RWKV_HEAD), lambda i: (0, 0, 0, 0))
    y, s = pl.pallas_call(
        _rwkv_step_body,
        out_shape=[jax.ShapeDtypeStruct((m, RWKV_HEADS, 1, RWKV_HEAD), f32), jax.ShapeDtypeStruct(s0.shape, f32)],
        grid=(m // bs,),
        in_specs=[vec] * 7 + [st, par, par, par],
        out_specs=[vec, st],
        compiler_params=_cparams("parallel"),
        name=name,
    )(*[h4(x) for x in gates], s0, h4(prm["r_k"]), h4(prm["ln_w"]), h4(prm["ln_b"]))
    return y.reshape(m, RWKV_W), s


def _gla_log_gate(ggd, wgup, bg):
    x = _dot(ggd.astype(bf16), wgup) + bg
    return -_softplus(-x) / GLA_GATE_NORM


def _gla_out(o, gain, gog):
    return _rms(o, gain) * (gog * _sigmoid(gog))


def _gla_scan_body(q_ref, k_ref, v_ref, ggd_ref, gog_ref, s0_ref, wgup_ref, bg_ref, gn_ref,
                   d_ref, s_out_ref, s_ref):
    c = pl.program_id(1)
    C = q_ref.shape[0]

    @pl.when(c == 0)
    def _():
        s_ref[...] = s0_ref[0]

    rowi = lax.broadcasted_iota(jnp.int32, (C, C), 0)
    coli = lax.broadcasted_iota(jnp.int32, (C, C), 1)
    incl = coli <= rowi
    lower = incl.astype(bf16)
    eye = (lax.broadcasted_iota(jnp.int32, (GLA_DK, GLA_DK), 0)
           == lax.broadcasted_iota(jnp.int32, (GLA_DK, GLA_DK), 1)).astype(f32)
    la = _gla_log_gate(ggd_ref[...], wgup_ref[...], bg_ref[...])
    b = _split_dot_left(lower, la)
    mid = (C // 2) // 8 * 8
    b_mid = b[mid:mid + 1, :]
    b_last = b[C - 1:C, :]
    qs = q_ref[...] * GLA_DK ** -0.5
    k, v, gog = k_ref[...], v_ref[...], gog_ref[...]
    q_inter = (qs * jnp.exp(b)).astype(bf16)
    q_mid = (qs * jnp.exp(b - b_mid)).astype(bf16)
    k_mid = (k * jnp.exp(b_mid - b)).astype(bf16)
    k_end = (k * jnp.exp(b_last - b)).astype(bf16)
    e_last = jnp.exp(b_last)
    heads = range(GLA_HEADS)
    ks = [slice(h * GLA_DK, (h + 1) * GLA_DK) for h in heads]
    vs = [slice(h * GLA_DV, (h + 1) * GLA_DV) for h in heads]
    s_old = [s_ref[h] for h in heads]
    vh = [v[:, vs[h]].astype(bf16) for h in heads]
    att = [jnp.where(incl, _dot_nt(q_mid[:, ks[h]], k_mid[:, ks[h]]), 0.0).astype(bf16) for h in heads]
    o = [_dot(q_inter[:, ks[h]], s_old[h].astype(bf16)) + _dot(att[h], vh[h]) for h in heads]
    e_col = [jnp.sum(eye * e_last[:, ks[h]], axis=-1, keepdims=True) for h in heads]
    for h in heads:
        s_ref[h] = s_old[h] * e_col[h] + _dot_tn(k_end[:, ks[h]], vh[h])
    d_ref[...] = jnp.concatenate([_gla_out(o[h], gn_ref[...], gog[:, vs[h]]) for h in heads], axis=1)

    @pl.when(c == pl.num_programs(1) - 1)
    def _():
        s_out_ref[0] = s_ref[...]


def gla_scan(cols1, s0, prm, batch, seq, *, name):
    C = min(GLA_CHUNK, seq)
    nch = seq // C
    kw, vw = GLA_HEADS * GLA_DK, GLA_HEADS * GLA_DV
    tok = lambda w, off: pl.BlockSpec((C, w), lambda b, c, cb=off // w: (b * nch + c, cb))
    st = pl.BlockSpec((1, GLA_HEADS, GLA_DK, GLA_DV), lambda b, c: (b, 0, 0, 0))
    full = lambda a: pl.BlockSpec(a.shape, lambda b, c: (0, 0))
    return pl.pallas_call(
        _gla_scan_body,
        out_shape=[jax.ShapeDtypeStruct((batch * seq, vw), f32),
                   jax.ShapeDtypeStruct((batch, GLA_HEADS, GLA_DK, GLA_DV), f32)],
        grid=(batch, nch),
        in_specs=[tok(kw, C1_GQ), tok(kw, C1_GK), tok(vw, C1_GV), tok(LANES, C1_GGD), tok(vw, C1_GOG), st,
                  full(prm["wgup"]), full(prm["bg"]), full(prm["gn"])],
        out_specs=[pl.BlockSpec((C, vw), lambda b, c: (b * nch + c, 0)), st],
        scratch_shapes=[pltpu.VMEM((GLA_HEADS, GLA_DK, GLA_DV), f32)],
        compiler_params=_cparams("parallel", "arbitrary"),
        name=name,
    )(cols1, cols1, cols1, cols1, cols1, s0, prm["wgup"], prm["bg"], prm["gn"])


def _gla_gate_body(ggd_ref, wgup_ref, bg_ref, la_ref):
    la_ref[...] = _gla_log_gate(ggd_ref[...], wgup_ref[...], bg_ref[...])


def gla_gate(cols1, prm, *, name):
    m = cols1.shape[0]
    kw = GLA_HEADS * GLA_DK
    return pl.pallas_call(
        _gla_gate_body,
        out_shape=jax.ShapeDtypeStruct((m, kw), f32),
        grid=(1,),
        in_specs=[pl.BlockSpec((m, LANES), lambda i: (0, C1_GGD // LANES)),
                  pl.BlockSpec(prm["wgup"].shape, lambda i: (0, 0)), pl.BlockSpec(prm["bg"].shape, lambda i: (0, 0))],
        out_specs=pl.BlockSpec((m, kw), lambda i: (0, 0)),
        compiler_params=_cparams("arbitrary"),
        name=name,
    )(cols1, prm["wgup"], prm["bg"])


def _gla_step_body(q_ref, k_ref, la_ref, v_ref, gog_ref, s0_ref, gn_ref, d_ref, s_out_ref):
    n = GLA_DK
    eye = (lax.broadcasted_iota(jnp.int32, (n, n), 0) == lax.broadcasted_iota(jnp.int32, (n, n), 1)).astype(f32)
    col = lambda x: jnp.sum(eye * x, axis=-1, keepdims=True)
    qc = col(q_ref[...] * GLA_DK ** -0.5)
    kc = col(k_ref[...])
    ec = col(jnp.exp(la_ref[...]))
    s = s0_ref[...] * ec + kc * v_ref[...]
    s_out_ref[...] = s
    o = jnp.sum(qc * s, axis=-2, keepdims=True)
    d_ref[...] = _gla_out(o, gn_ref[...], gog_ref[...])


def gla_step(q, k, la, v, gog, s0, prm, *, bs=8, name):
    m = s0.shape[0]
    bs = min(bs, m)
    assert m % bs == 0
    hk = lambda x: x.reshape(m, GLA_HEADS, 1, GLA_DK)
    hv = lambda x: x.reshape(m, GLA_HEADS, 1, GLA_DV)
    ks = pl.BlockSpec((bs, GLA_HEADS, 1, GLA_DK), lambda i: (i, 0, 0, 0))
    vs = pl.BlockSpec((bs, GLA_HEADS, 1, GLA_DV), lambda i: (i, 0, 0, 0))
    st = pl.BlockSpec((bs, GLA_HEADS, GLA_DK, GLA_DV), lambda i: (i, 0, 0, 0))
    gn = prm["gn"].reshape(1, 1, 1, GLA_DV)
    d, s = pl.pallas_call(
        _gla_step_body,
        out_shape=[jax.ShapeDtypeStruct((m, GLA_HEADS, 1, GLA_DV), f32), jax.ShapeDtypeStruct(s0.shape, f32)],
        grid=(m // bs,),
        in_specs=[ks, ks, ks, vs, vs, st, pl.BlockSpec(gn.shape, lambda i: (0, 0, 0, 0))],
        out_specs=[vs, st],
        compiler_params=_cparams("parallel"),
        name=name,
    )(hk(q), hk(k), hk(la), hv(v), hv(gog), s0, gn)
    return d.reshape(m, GLA_HEADS * GLA_DV), s


def _kv_token_minor_body(k_ref, v_ref, ko_ref, vo_ref):
    rows = k_ref.shape[0]
    ko_ref[0] = k_ref[...].T.reshape(SB_HEADS, SB_HEAD, rows)
    vo_ref[0] = v_ref[...].T.reshape(SB_HEADS, SB_HEAD, rows)


def kv_token_minor(cols1, groups, rows, *, tr=512, name):
    tr = min(tr, rows)
    assert rows % tr == 0
    nblk = rows // tr
    src = lambda off: pl.BlockSpec((tr, SB_W), lambda g, t, cb=off // SB_W: (g * nblk + t, cb))
    dst = pl.BlockSpec((1, SB_HEADS, SB_HEAD, tr), lambda g, t: (g, 0, 0, t))
    shape = jax.ShapeDtypeStruct((groups, SB_HEADS, SB_HEAD, rows), f32)
    return pl.pallas_call(
        _kv_token_minor_body,
        out_shape=[shape, shape],
        grid=(groups, nblk),
        in_specs=[src(C1_SK), src(C1_SV)],
        out_specs=[dst, dst],
        compiler_params=_cparams("parallel", "parallel"),
        name=name,
    )(cols1, cols1)


def _mla_decode_body(*refs, pages):
    pt_ref, q_ref, knew_ref = refs[:3]
    ckv_refs = refs[3:3 + pages]
    kpe_refs = refs[3 + pages:3 + 2 * pages]
    o_ref, m_ref, l_ref, acc_ref = refs[3 + 2 * pages:]
    j = pl.program_id(1)
    q = q_ref[0]
    q_abs, q_pe = q[:, :MLA_KV_RANK], q[:, MLA_KV_RANK:MLA_KV_RANK + MLA_ROPE]

    groups = m_ref.shape[0]
    per = pages // groups

    @pl.when(j == 0)
    def _():
        kn = knew_ref[0].astype(bf16).astype(f32)
        s_new = jnp.sum(q.astype(f32) * kn, axis=-1, keepdims=True) * MLA_SCALE
        m_ref[...] = jnp.full_like(m_ref, -jnp.inf)
        l_ref[...] = jnp.zeros_like(l_ref)
        acc_ref[...] = jnp.zeros_like(acc_ref)
        m_ref[0] = s_new
        l_ref[0] = jnp.ones_like(s_new)
        acc_ref[0] = jnp.broadcast_to(kn[:, :MLA_KV_RANK], acc_ref.shape[1:])

    cks = [r[0].astype(bf16) for r in ckv_refs]
    scores = [(_dot_nt(q_abs, ck) + _dot(q_pe, kr[0].astype(bf16))) * MLA_SCALE
              for ck, kr in zip(cks, kpe_refs)]
    gs = range(groups)
    s = [jnp.concatenate(scores[g * per:(g + 1) * per], axis=1) for g in gs]
    m_prev = [m_ref[g] for g in gs]
    m_new = [jnp.maximum(m_prev[g], jnp.max(s[g], axis=-1, keepdims=True)) for g in gs]
    alpha = [jnp.exp(m_prev[g] - m_new[g]) for g in gs]
    pr = [jnp.exp(s[g] - m_new[g]) for g in gs]
    for g in gs:
        l_ref[g] = l_ref[g] * alpha[g] + jnp.sum(pr[g], axis=-1, keepdims=True)
        acc = acc_ref[g] * alpha[g]
        for i in range(per):
            acc = acc + _dot(pr[g][:, i * PAGE_SIZE:(i + 1) * PAGE_SIZE].astype(bf16), cks[g * per + i])
        acc_ref[g] = acc
        m_ref[g] = m_new[g]

    @pl.when(j == pl.num_programs(1) - 1)
    def _():
        m_all = m_ref[0]
        for g in range(1, groups):
            m_all = jnp.maximum(m_all, m_ref[g])
        scale = [jnp.exp(m_ref[g] - m_all) for g in gs]
        num = sum(acc_ref[g] * scale[g] for g in gs)
        den = sum(l_ref[g] * scale[g] for g in gs)
        o_ref[0] = num / den


def mla_decode(qcat, knew, cache_ckv, cache_kpe, page_table, *, pages=64, groups=2, name):
    nseq, npages = page_table.shape
    pages = min(pages, npages)
    groups = min(groups, pages)
    assert npages % pages == 0 and pages % groups == 0
    q3 = qcat.reshape(nseq, MLA_HEADS, ABS_SLOT)
    kn3 = knew.reshape(nseq, 1, ABS_SLOT)
    kpe_t = jnp.transpose(cache_kpe, (0, 2, 1))
    page = lambda r, c, i: pl.BlockSpec((1, r, c), lambda b, j, pt, i=i: (pt[b, j * pages + i], 0, 0))
    grid_spec = pltpu.PrefetchScalarGridSpec(
        num_scalar_prefetch=1,
        grid=(nseq, npages // pages),
        in_specs=[pl.BlockSpec((1, MLA_HEADS, ABS_SLOT), lambda b, j, pt: (b, 0, 0)),
                  pl.BlockSpec((1, 1, ABS_SLOT), lambda b, j, pt: (b, 0, 0))]
        + [page(PAGE_SIZE, MLA_KV_RANK, i) for i in range(pages)]
        + [page(MLA_ROPE, PAGE_SIZE, i) for i in range(pages)],
        out_specs=pl.BlockSpec((1, MLA_HEADS, MLA_KV_RANK), lambda b, j, pt: (b, 0, 0)),
        scratch_shapes=[pltpu.VMEM((groups, MLA_HEADS, 1), f32), pltpu.VMEM((groups, MLA_HEADS, 1), f32),
                        pltpu.VMEM((groups, MLA_HEADS, MLA_KV_RANK), f32)],
    )
    lat = pl.pallas_call(
        functools.partial(_mla_decode_body, pages=pages),
        out_shape=jax.ShapeDtypeStruct((nseq, MLA_HEADS, MLA_KV_RANK), f32),
        grid_spec=grid_spec,
        compiler_params=_cparams("parallel", "arbitrary"),
        name=name,
    )(page_table, q3, kn3, *([cache_ckv] * pages), *([kpe_t] * pages))
    return lat.reshape(nseq, MLA_HEADS * MLA_KV_RANK)


def _sb_decode_body(*refs, pages):
    pt_ref, q_ref, upper_ref = refs[:3]
    k_refs = refs[3:3 + pages]
    v_refs = refs[3 + pages:3 + 2 * pages]
    o_ref, qb_ref, acc_ref, run_ref = refs[3 + 2 * pages:]
    j = pl.program_id(1)
    n = SB_HEAD
    eye = (lax.broadcasted_iota(jnp.int32, (n, n), 0) == lax.broadcasted_iota(jnp.int32, (n, n), 1)).astype(f32)

    @pl.when(j == 0)
    def _():
        q = q_ref[0]
        qcol = jnp.sum(eye[None] * q[:, None, :], axis=-1, keepdims=True)
        qb_ref[...] = jnp.broadcast_to(qcol, qb_ref.shape)
        acc_ref[...] = jnp.zeros_like(acc_ref)
        run_ref[...] = jnp.zeros_like(run_ref)

    qb = qb_ref[...]
    z = jnp.concatenate([jnp.sum(qb * k_ref[0], axis=1) for k_ref in k_refs], axis=0) * SB_SCALE
    log_fail = -_softplus(z)
    after = _split_dot_right(log_fail, upper_ref[...])
    total = jnp.sum(log_fail, axis=-1, keepdims=True)
    run = run_ref[...]
    ws = []
    for p in range(pages):
        rows = slice(p * SB_HEADS, (p + 1) * SB_HEADS)
        ws.append(jnp.exp(z[rows] + log_fail[rows] + after[rows] + run))
        run = run + total[rows]
    run_ref[...] = run
    for h in range(SB_HEADS):
        a = acc_ref[h]
        for p in range(pages):
            a = a + v_refs[p][0, h] * ws[p][h:h + 1, :]
        acc_ref[h] = a

    @pl.when(j == pl.num_programs(1) - 1)
    def _():
        ocol = jnp.sum(acc_ref[...], axis=-1, keepdims=True)
        o_ref[0] = jnp.sum(eye[None] * ocol, axis=1)


def sb_decode(q, cache_k, cache_v, page_table, *, pages=16, name):
    nseq, npages = page_table.shape
    pages = min(pages, npages)
    assert npages % pages == 0
    ck = jnp.transpose(cache_k, (0, 2, 3, 1))
    cv = jnp.transpose(cache_v, (0, 2, 3, 1))
    q3 = q.reshape(nseq, SB_HEADS, SB_HEAD)
    upper = (lax.broadcasted_iota(jnp.int32, (PAGE_SIZE, PAGE_SIZE), 0)
             > lax.broadcasted_iota(jnp.int32, (PAGE_SIZE, PAGE_SIZE), 1)).astype(bf16)
    page = lambda i: pl.BlockSpec(
        (1, SB_HEADS, SB_HEAD, PAGE_SIZE), lambda b, j, pt, i=i: (pt[b, npages - 1 - (j * pages + i)], 0, 0, 0))
    state = pltpu.VMEM((SB_HEADS, SB_HEAD, PAGE_SIZE), f32)
    grid_spec = pltpu.PrefetchScalarGridSpec(
        num_scalar_prefetch=1,
        grid=(nseq, npages // pages),
        in_specs=[pl.BlockSpec((1, SB_HEADS, SB_HEAD), lambda b, j, pt: (b, 0, 0)),
                  pl.BlockSpec((PAGE_SIZE, PAGE_SIZE), lambda b, j, pt: (0, 0))]
        + [page(i) for i in range(pages)] * 2,
        out_specs=pl.BlockSpec((1, SB_HEADS, SB_HEAD), lambda b, j, pt: (b, 0, 0)),
        scratch_shapes=[state, state, pltpu.VMEM((SB_HEADS, 1), f32)],
    )
    out = pl.pallas_call(
        functools.partial(_sb_decode_body, pages=pages),
        out_shape=jax.ShapeDtypeStruct((nseq, SB_HEADS, SB_HEAD), f32),
        grid_spec=grid_spec,
        compiler_params=_cparams("parallel", "arbitrary"),
        name=name,
    )(page_table, q3, upper, *([ck] * pages), *([cv] * pages))
    return out.reshape(nseq, SB_W)


def _place(width, pieces, dtype=None):
    if dtype is not None:
        pieces = [(off, arr.astype(dtype)) for off, arr in pieces]
    lead = pieces[0][1].shape[:-1]
    out, pos = [], 0
    for off, arr in pieces:
        if off > pos:
            out.append(jnp.zeros(lead + (off - pos,), arr.dtype))
        out.append(arr)
        pos = off + arr.shape[-1]
    if width > pos:
        out.append(jnp.zeros(lead + (width - pos,), pieces[0][1].dtype))
    return jnp.concatenate(out, axis=-1)


def _rot_half_cols(w):
    half = w.shape[-1] // 2
    return jnp.concatenate([-w[..., half:], w[..., :half]], axis=-1)


def _rwkv_col_pieces(x):
    w = RWKV_W
    return [(C0_R, x[..., :3 * w]), (C0_WDAD, x[..., 3 * w:3 * w + 128]), (C0_GD, x[..., 3 * w + 128:])]


def _prepare(p):
    out = {}
    mla_cols = 2 * MLA_Q_RANK + MLA_ROPE
    w_in0 = p["w_in0"]
    w_kpe = w_in0[:, 2 * MLA_Q_RANK:mla_cols]
    out["w_in0"] = _place(C0_N, _rwkv_col_pieces(w_in0[:, mla_cols:])[:1] + [
        (C0_QA, w_in0[:, :2 * MLA_Q_RANK]), (C0_KPE, w_kpe), (C0_KPEROT, _rot_half_cols(w_kpe)),
    ] + _rwkv_col_pieces(w_in0[:, mla_cols:])[1:], bf16)
    d_qk = MLA_NOPE + MLA_ROPE
    wqb = p["mla_w_qb"].reshape(MLA_Q_RANK, MLA_HEADS, d_qk)
    nope = wqb[:, :, :MLA_NOPE]
    rope = wqb[:, :, MLA_NOPE:]
    pad = jnp.zeros((MLA_Q_RANK, MLA_HEADS, LANES - MLA_ROPE), f32)
    flat = lambda x: x.reshape(MLA_Q_RANK, -1)
    out["wq_all"] = jnp.concatenate(
        [flat(nope), flat(jnp.concatenate([rope, pad], -1)), flat(jnp.concatenate([_rot_half_cols(rope), pad], -1))],
        axis=1).astype(bf16)
    out["w_kv"] = jnp.concatenate([p["mla_w_uk"].reshape(MLA_KV_RANK, -1), p["mla_w_uv"].reshape(MLA_KV_RANK, -1)],
                                  axis=1).astype(bf16)
    out["w_uk_t"] = jnp.transpose(p["mla_w_uk"], (1, 2, 0)).astype(bf16)
    out["w_uv_h"] = jnp.transpose(p["mla_w_uv"], (1, 0, 2)).astype(bf16)
    mu = _place(C0_N, _rwkv_col_pieces(p["rwkv_mu"].reshape(1, -1)))
    zrows = lambda n: jnp.zeros((n, RWKV_W), f32)
    head_id = jnp.arange(RWKV_W, dtype=jnp.int32) // RWKV_HEAD
    out["rwkv"] = {
        "mu_r": mu[:, C0_R:C0_R + RWKV_W], "mu_k": mu[:, C0_K:C0_K + RWKV_W], "mu_v": mu[:, C0_V:C0_V + RWKV_W],
        "mu_wdad": mu[:, C0_WDAD:C0_WDAD + LANES], "mu_gd": mu[:, C0_GD:C0_GD + 2 * LANES],
        "w0": p["rwkv_w0"].reshape(1, -1), "a0": p["rwkv_a0"].reshape(1, -1),
        "w2p": jnp.concatenate([p["rwkv_w2"], zrows(LANES - RWKV_DECAY_RANK)], 0).astype(bf16),
        "a2p": jnp.concatenate([zrows(RWKV_DECAY_RANK), p["rwkv_a2"]], 0).astype(bf16),
        "g2p": jnp.concatenate([p["rwkv_g2"], zrows(2 * LANES - RWKV_GATE_RANK)], 0).astype(bf16),
        "k_k": p["rwkv_k_k"].reshape(1, -1), "k_a": p["rwkv_k_a"].reshape(1, -1),
        "head_ones": (head_id[:, None] == head_id[None, :]).astype(bf16),
        "r_k": p["rwkv_r_k"].reshape(1, -1), "ln_w": p["rwkv_ln_w"].reshape(1, -1), "ln_b": p["rwkv_ln_b"].reshape(1, -1),
    }
    w_in1 = p["w_in1"]
    ggd0 = 3 * SB_W + 2 * GLA_HEADS * GLA_DK + GLA_HEADS * GLA_DV
    out["w_in1"] = _place(C1_N, [(0, w_in1[:, :ggd0]), (C1_GOG, w_in1[:, ggd0 + GLA_GATE_RANK:]),
                                 (C1_GGD, w_in1[:, ggd0:ggd0 + GLA_GATE_RANK])], bf16)
    out["gla"] = {
        "wgup": jnp.concatenate([p["gla_w_gup"], jnp.zeros((LANES - GLA_GATE_RANK, GLA_HEADS * GLA_DK), f32)],
                                0).astype(bf16),
        "bg": p["gla_b_g"].reshape(1, -1), "gn": p["gla_norm"].reshape(1, -1),
    }
    for name in ("w_out0", "w_out1", "ffn_w_gate", "ffn_w_up", "ffn_w_down"):
        out[name] = p[name].astype(bf16)
    return out


def _rope_tables(pos):
    half = MLA_ROPE // 2
    inv = 1.0 / (ROPE_THETA ** (jnp.arange(half, dtype=f32) / half))
    ang = pos.astype(f32)[:, None] * inv[None, :]
    cos, sin = jnp.cos(ang), jnp.sin(ang)
    return jnp.concatenate([cos] * 4, axis=1), jnp.concatenate([sin] * 4, axis=1)


def _rwkv_shift_row(cols0_row):
    return jnp.concatenate([cols0_row[..., C0_R:C0_R + 3 * RWKV_W], cols0_row[..., C0_WDAD:C0_WDAD + LANES],
                            cols0_row[..., C0_GD:C0_GD + RWKV_GATE_RANK]], axis=-1)


def _trunk(x, pos, past, p, w, batch, seq, tag):
    m = batch * seq
    cos, sin = _rope_tables(pos)
    cos, sin = jnp.tile(cos, (batch, 1)), jnp.tile(sin, (batch, 1))
    state = {}
    tmm = 1024 if m >= 1024 else m

    cols0 = matmul([(x, x.shape[1], 0)], [w["w_in0"]], gain=p["norm_mix"][0], tm=tmm, name=f"{tag}_in0")
    if past is None:
        qcat, kcat, v, ckv, kpe = mla_prep(cols0, cos, sin, p["mla_q_norm"], w["wq_all"], p["mla_kv_norm"],
                                           w["w_kv"], absorbed=False, name=f"{tag}_mla_prep")
        a_out = mla_flash(qcat, kcat, v, batch, seq, name=f"{tag}_mla_attn")
        shift_p = jnp.zeros((batch, C0_N), f32)
        s0 = jnp.zeros((batch, RWKV_HEADS, RWKV_HEAD, RWKV_HEAD), f32)
        gates = rwkv_gates(cols0, shift_p, seq, w["rwkv"], name=f"{tag}_rwkv_gates")
        b_out, s_rwkv = rwkv_scan(gates, s0, w["rwkv"], batch, seq, name=f"{tag}_rwkv_scan")
    else:
        qcat, ckv, kpe, knew = mla_prep(cols0, cos, sin, p["mla_q_norm"], w["wq_all"], p["mla_kv_norm"],
                                        w["w_uk_t"], absorbed=True, name=f"{tag}_mla_prep")
        lat = mla_decode(qcat, knew, past["mla_ckv"], past["mla_kpe"], past["page_table"], name=f"{tag}_mla_attn")
        a_out = head_matmul(lat, w["w_uv_h"], name=f"{tag}_mla_uv")
        shift_p = _place(C0_N, _rwkv_col_pieces(past["rwkv_shift"]))
        gates = rwkv_gates(cols0, shift_p, 1, w["rwkv"], name=f"{tag}_rwkv_gates")
        b_out, s_rwkv = rwkv_step(gates, past["rwkv_state"], w["rwkv"], name=f"{tag}_rwkv_step")
    state.update(mla_ckv=ckv.reshape(batch, seq, MLA_KV_RANK), mla_kpe=kpe.reshape(batch, seq, MLA_ROPE),
                 rwkv_state=s_rwkv, rwkv_shift=_rwkv_shift_row(cols0.reshape(batch, seq, C0_N)[:, -1]))
    half = w["w_out0"].shape[0] // 2
    x = matmul([(a_out, half, 0), (b_out, half, 0)], [w["w_out0"][:half], w["w_out0"][half:]], res=x, tm=tmm,
               name=f"{tag}_out0")
    x = ffn(x, p["norm_ffn"][0], w["ffn_w_gate"][0], w["ffn_w_up"][0], w["ffn_w_down"][0], name=f"{tag}_ffn0")

    cols1 = matmul([(x, x.shape[1], 0)], [w["w_in1"]], gain=p["norm_mix"][1], tm=tmm, name=f"{tag}_in1")
    if past is None:
        c_out = sb_flash(cols1, batch, seq, name=f"{tag}_sb_attn")
        g0 = jnp.zeros((batch, GLA_HEADS, GLA_DK, GLA_DV), f32)
        d_out, s_gla = gla_scan(cols1, g0, w["gla"], batch, seq, name=f"{tag}_gla_scan")
    else:
        c_out = sb_decode(cols1[:, C1_SQ:C1_SQ + SB_W], past["sb_k"], past["sb_v"], past["page_table"],
                          name=f"{tag}_sb_attn")
        la = gla_gate(cols1, w["gla"], name=f"{tag}_gla_gate")
        kw, vw = GLA_HEADS * GLA_DK, GLA_HEADS * GLA_DV
        d_out, s_gla = gla_step(cols1[:, C1_GQ:C1_GQ + kw], cols1[:, C1_GK:C1_GK + kw], la,
                                cols1[:, C1_GV:C1_GV + vw], cols1[:, C1_GOG:C1_GOG + vw], past["gla_state"],
                                w["gla"], name=f"{tag}_gla_step")
    if past is None:
        k_t, v_t = kv_token_minor(cols1, batch, seq, name=f"{tag}_sb_kv")
        perm = (0, 3, 1, 2)
    else:
        k_t, v_t = kv_token_minor(cols1, 1, batch, name=f"{tag}_sb_kv")
        perm = (3, 0, 1, 2)
    state.update(sb_k=jnp.transpose(k_t, perm), sb_v=jnp.transpose(v_t, perm), gla_state=s_gla)
    half = w["w_out1"].shape[0] // 2
    x = matmul([(c_out, half, 0), (d_out, half, 0)], [w["w_out1"][:half], w["w_out1"][half:]], res=x, tm=tmm,
               name=f"{tag}_out1")
    y = ffn(x, p["norm_ffn"][1], w["ffn_w_gate"][1], w["ffn_w_up"][1], w["ffn_w_down"][1],
            final_gain=p["norm_final"], name=f"{tag}_ffn1")
    return y.reshape(batch, seq, -1), state


def kernel(x_prompt, x_sample, cache_mla_ckv, cache_mla_kpe, cache_sb_k, cache_sb_v, state_rwkv, state_rwkv_shift, state_gla, page_table, w_in0, mla_q_norm, mla_w_qb, mla_kv_norm, mla_w_uk, mla_w_uv, rwkv_mu, rwkv_w0, rwkv_w2, rwkv_a0, rwkv_a2, rwkv_g2, rwkv_k_k, rwkv_k_a, rwkv_r_k, rwkv_ln_w, rwkv_ln_b, w_out0, w_in1, gla_w_gup, gla_b_g, gla_norm, w_out1, norm_mix, norm_ffn, ffn_w_gate, ffn_w_up, ffn_w_down, norm_final):
    p = dict(w_in0=w_in0, mla_q_norm=mla_q_norm, mla_w_qb=mla_w_qb, mla_kv_norm=mla_kv_norm, mla_w_uk=mla_w_uk,
             mla_w_uv=mla_w_uv, rwkv_mu=rwkv_mu, rwkv_w0=rwkv_w0, rwkv_w2=rwkv_w2, rwkv_a0=rwkv_a0,
             rwkv_a2=rwkv_a2, rwkv_g2=rwkv_g2, rwkv_k_k=rwkv_k_k, rwkv_k_a=rwkv_k_a, rwkv_r_k=rwkv_r_k,
             rwkv_ln_w=rwkv_ln_w, rwkv_ln_b=rwkv_ln_b, w_out0=w_out0, w_in1=w_in1, gla_w_gup=gla_w_gup,
             gla_b_g=gla_b_g, gla_norm=gla_norm, w_out1=w_out1, norm_mix=norm_mix, norm_ffn=norm_ffn,
             ffn_w_gate=ffn_w_gate, ffn_w_up=ffn_w_up, ffn_w_down=ffn_w_down, norm_final=norm_final)
    w = _prepare(p)
    b, t, d = x_prompt.shape
    db, dt, _ = x_sample.shape
    assert dt == 1
    past_len = page_table.shape[1] * PAGE_SIZE
    y_p, sp = _trunk(x_prompt.reshape(b * t, d), jnp.arange(t, dtype=jnp.int32), None, p, w, b, t, "p")
    past = dict(mla_ckv=cache_mla_ckv, mla_kpe=cache_mla_kpe, sb_k=cache_sb_k, sb_v=cache_sb_v,
                rwkv_state=state_rwkv, rwkv_shift=state_rwkv_shift, gla_state=state_gla, page_table=page_table)
    y_s, ss = _trunk(x_sample.reshape(db, d), past_len + jnp.arange(1, dtype=jnp.int32), past, p, w, db, 1, "s")
    keys = ("mla_ckv", "mla_kpe", "rwkv_state", "rwkv_shift", "sb_k", "sb_v", "gla_state")
    return (y_p, y_s) + tuple(sp[k] for k in keys) + tuple(ss[k] for k in keys)
```
